```python
import jax, jax.numpy as jnp
from jax import lax
import numpy as np

D_MODEL = 2048
BATCH = 4
SEQ = 2048
DEPTH = 4
DEC_BATCH = 128
DEC_SEQ = 1
PAST_LEN = 8192
PAGE_SIZE = 128

N_MIXERS = 3
N_CONF_LAYERS = (DEPTH + N_MIXERS - 1) // N_MIXERS
CONF_KERNEL = 31
SC_KERNEL = 3
N_HEADS = 16
QK_NOPE_DIM = 128
QK_ROPE_DIM = 64
V_HEAD_DIM = 128
KV_LORA_RANK = 512
Q_LORA_RANK = 512
ROPE_THETA = 10000.0
ATTN_SCALE = (QK_NOPE_DIM + QK_ROPE_DIM) ** -0.5
FFN_HIDDEN = ((8 * D_MODEL + 3 * 256 - 1) // (3 * 256)) * 256
Q_BLOCK = 128
NORM_EPS = 1e-6

kernel_name = 'hybrid_conformer_mla_shortconv_decoder_step'


def rms_norm(x, g):
    xf = x.astype(jnp.float32)
    y = xf * lax.rsqrt(jnp.mean(xf * xf, axis=-1, keepdims=True) + NORM_EPS)
    return (y * g.astype(jnp.float32)).astype(x.dtype)


def layer_norm(x, g, b):
    xf = x.astype(jnp.float32)
    mu = jnp.mean(xf, axis=-1, keepdims=True)
    var = jnp.mean(jnp.square(xf - mu), axis=-1, keepdims=True)
    y = (xf - mu) * lax.rsqrt(var + NORM_EPS) * g.astype(jnp.float32) + b.astype(jnp.float32)
    return y.astype(x.dtype)


def causal_dwconv(x_ext, w):
    return lax.conv_general_dilated(
        x_ext, w[:, None, :].astype(x_ext.dtype), (1,), 'VALID',
        dimension_numbers=('NWC', 'WIO', 'NWC'), feature_group_count=x_ext.shape[-1])


def rope(x, pos):
    half = x.shape[-1] // 2
    inv = ROPE_THETA ** (-jnp.arange(half, dtype=jnp.float32) / half)
    ang = pos.astype(jnp.float32)[:, None] * inv[None, :]
    cos = jnp.cos(ang)[:, None, :]
    sin = jnp.sin(ang)[:, None, :]
    xf = x.astype(jnp.float32)
    x1, x2 = xf[..., :half], xf[..., half:]
    return jnp.concatenate([x1 * cos - x2 * sin, x1 * sin + x2 * cos], axis=-1).astype(x.dtype)


def swiglu(h, w_gu, w_down):
    g, u = jnp.split(h @ w_gu, 2, axis=-1)
    return (jax.nn.silu(g) * u) @ w_down


def conformer_conv(h, prefix, w_in, b_in, dw, dw_b, ln_g, ln_b, w_out, b_out):
    a = h @ w_in + b_in
    u = a[..., :D_MODEL] * jax.nn.sigmoid(a[..., D_MODEL:])
    u_ext = jnp.concatenate([prefix.astype(u.dtype), u], axis=1)
    v = causal_dwconv(u_ext, dw) + dw_b
    v = jax.nn.silu(layer_norm(v, ln_g, ln_b))
    return v @ w_out + b_out, u_ext[:, -(CONF_KERNEL - 1):]


def short_conv(h, prefix, w_in, w_conv, w_out):
    bg, cg, hh = jnp.split(h @ w_in, 3, axis=-1)
    u = cg * hh
    u_ext = jnp.concatenate([prefix.astype(u.dtype), u], axis=1)
    v = causal_dwconv(u_ext, w_conv)
    return (bg * v) @ w_out, u_ext[:, -(SC_KERNEL - 1):]


def mla_project(h, pos, w_dq, q_norm, w_uq, w_dkv, kv_norm):
    b, t, _ = h.shape
    cq = rms_norm(h @ w_dq, q_norm)
    q = (cq @ w_uq).reshape(b, t, N_HEADS, QK_NOPE_DIM + QK_ROPE_DIM)
    q_nope = q[..., :QK_NOPE_DIM]
    q_pe = rope(q[..., QK_NOPE_DIM:], pos)
    kv = h @ w_dkv
    c_kv = rms_norm(kv[..., :KV_LORA_RANK], kv_norm)
    k_pe = rope(kv[..., None, KV_LORA_RANK:], pos)[:, :, 0]
    return q_nope, q_pe, c_kv, k_pe


def mla_prompt(q_nope, q_pe, c_kv, k_pe, w_uk, w_uv):
    b, s = c_kv.shape[:2]
    k_nope = jnp.einsum('bsr,rhd->bshd', c_kv, w_uk)
    v = jnp.einsum('bsr,rhd->bshd', c_kv, w_uv)
    nq = s // Q_BLOCK
    key_pos = jnp.arange(s)

    def split_blocks(a):
        return a.reshape(b, nq, Q_BLOCK, *a.shape[2:]).swapaxes(0, 1)

    def one_block(args):
        qn, qp, blk = args
        sc = (jnp.einsum('bqhd,bkhd->bhqk', qn, k_nope)
              + jnp.einsum('bqhp,bkp->bhqk', qp, k_pe)).astype(jnp.float32) * ATTN_SCALE
        q_pos = blk * Q_BLOCK + jnp.arange(Q_BLOCK)
        sc = jnp.where(key_pos[None, :] <= q_pos[:, None], sc, -jnp.inf)
        p = jax.nn.softmax(sc, axis=-1).astype(v.dtype)
        return jnp.einsum('bhqk,bkhd->bqhd', p, v)

    o = lax.map(one_block, (split_blocks(q_nope), split_blocks(q_pe), jnp.arange(nq)))
    return o.swapaxes(0, 1).reshape(b, s, N_HEADS * V_HEAD_DIM)


def mla_sample(q_nope, q_pe, c_new, kpe_new, cache_ckv, cache_kpe, page_table, w_uk, w_uv):
    b, t = q_nope.shape[:2]
    q_lat = jnp.einsum('bthd,rhd->bthr', q_nope, w_uk)

    def scores(ck, kp):
        return (jnp.einsum('bthr,bsr->bhts', q_lat, ck)
                + jnp.einsum('bthp,bsp->bhts', q_pe, kp)).astype(jnp.float32) * ATTN_SCALE

    def update(carry, s, ck):
        m, l, acc = carry
        m_new = jnp.maximum(m, s.max(axis=-1))
        corr = jnp.exp(m - m_new)
        p = jnp.exp(s - m_new[..., None])
        acc = acc * corr[..., None] + jnp.einsum('bhts,bsr->bhtr', p, ck.astype(jnp.float32))
        return (m_new, l * corr + p.sum(axis=-1), acc)

    def page_step(carry, pages):
        ck = cache_ckv[pages]
        kp = cache_kpe[pages]
        return update(carry, scores(ck, kp), ck), None

    init = (jnp.full((b, N_HEADS, t), -jnp.inf, jnp.float32),
            jnp.zeros((b, N_HEADS, t), jnp.float32),
            jnp.zeros((b, N_HEADS, t, KV_LORA_RANK), jnp.float32))
    carry, _ = lax.scan(page_step, init, page_table.T)
    causal = jnp.tril(jnp.ones((t, t), dtype=bool))
    s_self = jnp.where(causal, scores(c_new, kpe_new), -jnp.inf)
    _, l, acc = update(carry, s_self, c_new)
    o_lat = (acc / l[..., None]).astype(q_nope.dtype)
    o = jnp.einsum('bhtr,rhd->bthd', o_lat, w_uv)
    return o.reshape(b, t, N_HEADS * V_HEAD_DIM)


def setup_inputs(seed: int = 0) -> dict:
    key = jax.random.key(seed)
    ks = iter(jax.random.split(key, 48))
    n_pages = PAST_LEN // PAGE_SIZE
    n_pool = (DEC_BATCH * n_pages * 5) // 4

    def w(shape, fan_in):
        return jax.random.normal(next(ks), shape, jnp.float32) * fan_in ** -0.5

    def gain(shape):
        return 1.0 + 0.05 * jax.random.normal(next(ks), shape, jnp.float32)

    def bias(shape):
        return 0.02 * jax.random.normal(next(ks), shape, jnp.float32)

    def rnd(shape):
        return jax.random.normal(next(ks), shape, jnp.float32)

    page_table = jax.random.permutation(next(ks), n_pool)[: DEC_BATCH * n_pages]
    page_table = page_table.reshape(DEC_BATCH, n_pages).astype(jnp.int32)
    return {
        'x_prompt': rnd((BATCH, SEQ, D_MODEL)),
        'x_sample': rnd((DEC_BATCH, DEC_SEQ, D_MODEL)),
        'state_conv_l0': 0.5 * rnd((DEC_BATCH, CONF_KERNEL - 1, D_MODEL)),
        'cache_ckv_l1': rnd((n_pool, PAGE_SIZE, KV_LORA_RANK)),
        'cache_kpe_l1': rnd((n_pool, PAGE_SIZE, QK_ROPE_DIM)),
        'state_sconv_l2': rnd((DEC_BATCH, SC_KERNEL - 1, D_MODEL)),
        'state_conv_l3': 0.5 * rnd((DEC_BATCH, CONF_KERNEL - 1, D_MODEL)),
        'page_table': page_table,
        'norm_g': gain((DEPTH, 4, D_MODEL)),
        'ca_w_in': w((N_CONF_LAYERS, D_MODEL, 2 * D_MODEL), D_MODEL),
        'ca_b_in': bias((N_CONF_LAYERS, 2 * D_MODEL)),
        'ca_dw': w((N_CONF_LAYERS, CONF_KERNEL, D_MODEL), CONF_KERNEL),
        'ca_dw_b': bias((N_CONF_LAYERS, D_MODEL)),
        'ca_ln_g': gain((N_CONF_LAYERS, D_MODEL)),
        'ca_ln_b': bias((N_CONF_LAYERS, D_MODEL)),
        'ca_w_out': w((N_CONF_LAYERS, D_MODEL, D_MODEL), D_MODEL),
        'ca_b_out': bias((N_CONF_LAYERS, D_MODEL)),
        'mla_w_dq': w((D_MODEL, Q_LORA_RANK), D_MODEL),
        'mla_q_norm': gain((Q_LORA_RANK,)),
        'mla_w_uq': w((Q_LORA_RANK, N_HEADS * (QK_NOPE_DIM + QK_ROPE_DIM)), Q_LORA_RANK),
        'mla_w_dkv': w((D_MODEL, KV_LORA_RANK + QK_ROPE_DIM), D_MODEL),
        'mla_kv_norm': gain((KV_LORA_RANK,)),
        'mla_w_uk': w((KV_LORA_RANK, N_HEADS, QK_NOPE_DIM), KV_LORA_RANK),
        'mla_w_uv': w((KV_LORA_RANK, N_HEADS, V_HEAD_DIM), KV_LORA_RANK),
        'mla_w_o': w((N_HEADS * V_HEAD_DIM, D_MODEL), N_HEADS * V_HEAD_DIM),
        'sc_w_in': w((D_MODEL, 3 * D_MODEL), D_MODEL),
        'sc_w_conv': w((SC_KERNEL, D_MODEL), SC_KERNEL),
        'sc_w_out': w((D_MODEL, D_MODEL), D_MODEL),
        'ffn_w_gu': w((DEPTH, D_MODEL, 2 * FFN_HIDDEN), D_MODEL),
        'ffn_w_down': w((DEPTH, FFN_HIDDEN, D_MODEL), FFN_HIDDEN),
    }


def reference(x_prompt, x_sample, state_conv_l0, cache_ckv_l1, cache_kpe_l1, state_sconv_l2, state_conv_l3,
              page_table, norm_g, ca_w_in, ca_b_in, ca_dw, ca_dw_b, ca_ln_g, ca_ln_b, ca_w_out, ca_b_out,
              mla_w_dq, mla_q_norm, mla_w_uq, mla_w_dkv, mla_kv_norm, mla_w_uk, mla_w_uv, mla_w_o,
              sc_w_in, sc_w_conv, sc_w_out, ffn_w_gu, ffn_w_down):
    b_p, s_p = x_prompt.shape[:2]
    b_s, t_s = x_sample.shape[:2]
    pos_p = jnp.arange(s_p)
    pos_s = PAST_LEN + jnp.arange(t_s)
    conv_state_in = (state_conv_l0, state_conv_l3)
    new_conv = []
    xp, xs = x_prompt, x_sample
    for i in range(DEPTH):
        kind = i % N_MIXERS
        hp = rms_norm(xp, norm_g[i, 0])
        hs = rms_norm(xs, norm_g[i, 0])
        if kind == 0:
            j = i // N_MIXERS
            wts = (ca_w_in[j], ca_b_in[j], ca_dw[j], ca_dw_b[j], ca_ln_g[j], ca_ln_b[j], ca_w_out[j], ca_b_out[j])
            mp, cp = conformer_conv(hp, jnp.zeros((b_p, CONF_KERNEL - 1, D_MODEL), hp.dtype), *wts)
            ms, cs = conformer_conv(hs, conv_state_in[j], *wts)
            new_conv.append((cp, cs))
        elif kind == 1:
            proj = (mla_w_dq, mla_q_norm, mla_w_uq, mla_w_dkv, mla_kv_norm)
            qn_p, qp_p, ckv_p, kpe_p = mla_project(hp, pos_p, *proj)
            mp = mla_prompt(qn_p, qp_p, ckv_p, kpe_p, mla_w_uk, mla_w_uv) @ mla_w_o
            qn_s, qp_s, ckv_s, kpe_s = mla_project(hs, pos_s, *proj)
            ms = mla_sample(qn_s, qp_s, ckv_s, kpe_s, cache_ckv_l1, cache_kpe_l1, page_table,
                            mla_w_uk, mla_w_uv) @ mla_w_o
        else:
            mp, sc_p = short_conv(hp, jnp.zeros((b_p, SC_KERNEL - 1, D_MODEL), hp.dtype), sc_w_in, sc_w_conv, sc_w_out)
            ms, sc_s = short_conv(hs, state_sconv_l2, sc_w_in, sc_w_conv, sc_w_out)
        xp = xp + rms_norm(mp, norm_g[i, 1])
        xs = xs + rms_norm(ms, norm_g[i, 1])
        xp = xp + rms_norm(swiglu(rms_norm(xp, norm_g[i, 2]), ffn_w_gu[i], ffn_w_down[i]), norm_g[i, 3])
        xs = xs + rms_norm(swiglu(rms_norm(xs, norm_g[i, 2]), ffn_w_gu[i], ffn_w_down[i]), norm_g[i, 3])
    return (xp, xs, new_conv[0][0], new_conv[0][1], ckv_p, kpe_p, ckv_s, kpe_s, sc_p, sc_s,
            new_conv[1][0], new_conv[1][1])
```

```python
import functools

import jax
import jax.numpy as jnp
from jax import lax
from jax.experimental import pallas as pl
from jax.experimental.pallas import tpu as pltpu

F32 = jnp.float32
BF16 = jnp.bfloat16

N_HEADS = 16
QK_NOPE_DIM = 128
QK_ROPE_DIM = 64
V_HEAD_DIM = 128
KV_LORA_RANK = 512
ROPE_THETA = 10000.0
ATTN_SCALE = (QK_NOPE_DIM + QK_ROPE_DIM) ** -0.5
NORM_EPS = 1e-6
N_MIXERS = 3
LANES = 128

VMEM_LIMIT_BYTES = 56 * 1024 * 1024


def _params(n_grid_axes):
    return pltpu.CompilerParams(
        dimension_semantics=("arbitrary",) * n_grid_axes,
        vmem_limit_bytes=VMEM_LIMIT_BYTES)


def _sigmoid(x):
    return 1.0 / (1.0 + jnp.exp(-x))


def _rms(x, g):
    return x * lax.rsqrt(jnp.mean(x * x, axis=-1, keepdims=True) + NORM_EPS) * g


def _mm_kernel(n_w, n_ex, combine, x_ref, *refs):
    w_refs = refs[:n_w]
    ex_refs = refs[n_w:n_w + n_ex]
    out_refs = refs[n_w + n_ex:]
    x = x_ref[...]
    accs = [jnp.dot(x, w[...].astype(BF16), preferred_element_type=F32) for w in w_refs]
    outs = combine(accs, [e[...] for e in ex_refs])
    for o_ref, o in zip(out_refs, outs):
        o_ref[...] = o.astype(o_ref.dtype)


def multi_mm(name, x, ws, extras, combine, out_dtypes, *, n_cols, tm, tn):
    m, k = x.shape
    grid = (m // tm, n_cols // tn)
    in_specs = [pl.BlockSpec((tm, k), lambda i, j: (i, 0))]
    args = [x]
    for w, off in ws:
        in_specs.append(pl.BlockSpec((k, tn), lambda i, j, off=off: (0, j + off)))
        args.append(w)
    for arr, kind, off in extras:
        if kind == "row":
            in_specs.append(pl.BlockSpec((tm, arr.shape[1]), lambda i, j: (i, 0)))
        else:
            in_specs.append(pl.BlockSpec((1, tn), lambda i, j, off=off: (0, j + off)))
        args.append(arr)
    out_shape = [jax.ShapeDtypeStruct((m, n_cols), dt) for dt in out_dtypes]
    out_specs = [pl.BlockSpec((tm, tn), lambda i, j: (i, j)) for _ in out_dtypes]
    outs = pl.pallas_call(
        functools.partial(_mm_kernel, len(ws), len(extras), combine),
        grid=grid, in_specs=in_specs, out_specs=out_specs, out_shape=out_shape,
        compiler_params=_params(2), name=name)(*args)
    return outs


def _rowwise(name, fn, row_args, vec_args, out_dtypes_widths, *, tr):
    m = row_args[0].shape[0]
    in_specs = [pl.BlockSpec((tr, a.shape[1]), lambda i: (i, 0)) for a in row_args]
    in_specs += [pl.BlockSpec((1, v.shape[1]), lambda i: (0, 0)) for v in vec_args]
    out_shape = [jax.ShapeDtypeStruct((m, w), dt) for dt, w in out_dtypes_widths]
    out_specs = [pl.BlockSpec((tr, w), lambda i: (i, 0)) for _, w in out_dtypes_widths]
    n_r, n_v = len(row_args), len(vec_args)

    def body(*refs):
        rows = [r[...] for r in refs[:n_r]]
        vecs = [r[...] for r in refs[n_r:n_r + n_v]]
        outs = fn(rows, vecs)
        for o_ref, o in zip(refs[n_r + n_v:], outs):
            o_ref[...] = o.astype(o_ref.dtype)

    return pl.pallas_call(body, grid=(m // tr,), in_specs=in_specs, out_specs=out_specs,
                          out_shape=out_shape, compiler_params=_params(1), name=name)(
                              *row_args, *vec_args)


def _resid_norm_fn(rows, vecs):
    x, mix = rows
    x_new = x + _rms(mix, vecs[0])
    return x_new, _rms(x_new, vecs[1])


def _resid_fn(rows, vecs):
    x, mix = rows
    return (x + _rms(mix, vecs[0]),)


def _ffn_kernel(x_ref, wg_ref, wu_ref, wd_ref, o_ref):
    f = pl.program_id(1)
    x = x_ref[...]
    g = jnp.dot(x, wg_ref[...].astype(BF16), preferred_element_type=F32)
    u = jnp.dot(x, wu_ref[...].astype(BF16), preferred_element_type=F32)
    act = (g * _sigmoid(g) * u).astype(BF16)
    part = jnp.dot(act, wd_ref[...].astype(BF16), preferred_element_type=F32)

    @pl.when(f == 0)
    def _():
        o_ref[...] = part

    @pl.when(f > 0)
    def _():
        o_ref[...] += part


def ffn(x, w_gu, w_down, *, tm, tf):
    m, d = x.shape
    hidden = w_down.shape[0]
    n_f = hidden // tf
    return pl.pallas_call(
        _ffn_kernel, grid=(m // tm, n_f),
        in_specs=[pl.BlockSpec((tm, d), lambda i, f: (i, 0)),
                  pl.BlockSpec((d, tf), lambda i, f: (0, f)),
                  pl.BlockSpec((d, tf), lambda i, f: (0, f + n_f)),
                  pl.BlockSpec((tf, d), lambda i, f: (f, 0))],
        out_specs=pl.BlockSpec((tm, d), lambda i, f: (i, 0)),
        out_shape=jax.ShapeDtypeStruct((m, d), F32),
        compiler_params=_params(2), name="ffn")(x, w_gu, w_gu, w_down)


CONV_HALO = 32
CONV_ROWS = 32
CONV_COLS = 512


def _conf_conv_kernel(ts, n_taps, halo_ref, main_ref, w_ref, b_ref, g_ref, beta_ref, o_ref,
                      win_ref, conv_ref):
    t = pl.program_id(1)
    d = main_ref.shape[1]
    win_ref[0:CONV_HALO, :] = jnp.where(t == 0, 0.0, halo_ref[...])
    win_ref[CONV_HALO:, :] = main_ref[...]
    first = CONV_HALO - (n_taps - 1)
    for r0 in range(0, ts, CONV_ROWS):
        for c0 in range(0, d, CONV_COLS):
            acc = jnp.zeros((CONV_ROWS, CONV_COLS), F32)
            for k in range(n_taps):
                acc = acc + (w_ref[k:k + 1, c0:c0 + CONV_COLS]
                             * win_ref[r0 + first + k:r0 + first + k + CONV_ROWS, c0:c0 + CONV_COLS])
            conv_ref[r0:r0 + CONV_ROWS, c0:c0 + CONV_COLS] = acc + b_ref[:, c0:c0 + CONV_COLS]
    v = conv_ref[...]
    mu = jnp.mean(v, axis=-1, keepdims=True)
    var = jnp.mean(jnp.square(v - mu), axis=-1, keepdims=True)
    y = (v - mu) * lax.rsqrt(var + NORM_EPS) * g_ref[...] + beta_ref[...]
    o_ref[...] = (y * _sigmoid(y)).astype(o_ref.dtype)


def conf_conv_prompt(u, dw, dw_b, ln_g, ln_b, *, batch, seq, ts):
    d = u.shape[1]
    n_taps = dw.shape[0]
    n_t = seq // ts
    halo_per_tile = ts // CONV_HALO
    return pl.pallas_call(
        functools.partial(_conf_conv_kernel, ts, n_taps),
        grid=(batch, n_t),
        in_specs=[
            pl.BlockSpec((CONV_HALO, d),
                         lambda b, t: (jnp.maximum((b * n_t + t) * halo_per_tile - 1, 0), 0)),
            pl.BlockSpec((ts, d), lambda b, t: (b * n_t + t, 0)),
            pl.BlockSpec((n_taps, d), lambda b, t: (0, 0)),
            pl.BlockSpec((1, d), lambda b, t: (0, 0)),
            pl.BlockSpec((1, d), lambda b, t: (0, 0)),
            pl.BlockSpec((1, d), lambda b, t: (0, 0))],
        out_specs=pl.BlockSpec((ts, d), lambda b, t: (b * n_t + t, 0)),
        out_shape=jax.ShapeDtypeStruct((batch * seq, d), BF16),
        scratch_shapes=[pltpu.VMEM((ts + CONV_HALO, d), F32), pltpu.VMEM((ts, d), F32)],
        compiler_params=_params(2), name="conf_conv_prompt")(u, u, dw, dw_b, ln_g, ln_b)


SC_HALO = 8


def _short_conv_kernel(ts, n_taps, halo_ref, main_ref, gate_ref, w_ref, o_ref, win_ref):
    t = pl.program_id(1)
    win_ref[0:SC_HALO, :] = jnp.where(t == 0, 0.0, halo_ref[...])
    win_ref[SC_HALO:, :] = main_ref[...]
    first = SC_HALO - (n_taps - 1)
    acc = w_ref[0:1, :] * win_ref[first:first + ts, :]
    for k in range(1, n_taps):
        acc = acc + w_ref[k:k + 1, :] * win_ref[first + k:first + k + ts, :]
    o_ref[...] = (gate_ref[...] * acc).astype(o_ref.dtype)


def short_conv_prompt(u, gate, w_conv, *, batch, seq, ts):
    d = u.shape[1]
    n_taps = w_conv.shape[0]
    n_t = seq // ts
    halo_per_tile = ts // SC_HALO
    return pl.pallas_call(
        functools.partial(_short_conv_kernel, ts, n_taps),
        grid=(batch, n_t),
        in_specs=[
            pl.BlockSpec((SC_HALO, d),
                         lambda b, t: (jnp.maximum((b * n_t + t) * halo_per_tile - 1, 0), 0)),
            pl.BlockSpec((ts, d), lambda b, t: (b * n_t + t, 0)),
            pl.BlockSpec((ts, d), lambda b, t: (b * n_t + t, 0)),
            pl.BlockSpec((n_taps, d), lambda b, t: (0, 0))],
        out_specs=pl.BlockSpec((ts, d), lambda b, t: (b * n_t + t, 0)),
        out_shape=jax.ShapeDtypeStruct((batch * seq, d), BF16),
        scratch_shapes=[pltpu.VMEM((ts + SC_HALO, d), F32)],
        compiler_params=_params(2), name="short_conv_prompt")(u, u, gate, w_conv)


def _conf_conv_sample_kernel(n_hist, st_ref, u_ref, w_ref, b_ref, g_ref, beta_ref, o_ref):
    d = u_ref.shape[1]
    acc = w_ref[n_hist:n_hist + 1, :] * u_ref[...] + b_ref[...]
    for k in range(n_hist):
        acc = acc + w_ref[k:k + 1, :] * st_ref[:, k * d:(k + 1) * d]
    mu = jnp.mean(acc, axis=-1, keepdims=True)
    var = jnp.mean(jnp.square(acc - mu), axis=-1, keepdims=True)
    y = (acc - mu) * lax.rsqrt(var + NORM_EPS) * g_ref[...] + beta_ref[...]
    o_ref[...] = (y * _sigmoid(y)).astype(o_ref.dtype)


def conf_conv_sample(state, u_s, dw, dw_b, ln_g, ln_b, *, tb):
    n_b, n_hist, d = state.shape
    st2 = state.reshape(n_b, n_hist * d)
    return pl.pallas_call(
        functools.partial(_conf_conv_sample_kernel, n_hist),
        grid=(n_b // tb,),
        in_specs=[pl.BlockSpec((tb, n_hist * d), lambda i: (i, 0)),
                  pl.BlockSpec((tb, d), lambda i: (i, 0)),
                  pl.BlockSpec((n_hist + 1, d), lambda i: (0, 0)),
                  pl.BlockSpec((1, d), lambda i: (0, 0)),
                  pl.BlockSpec((1, d), lambda i: (0, 0)),
                  pl.BlockSpec((1, d), lambda i: (0, 0))],
        out_specs=pl.BlockSpec((tb, d), lambda i: (i, 0)),
        out_shape=jax.ShapeDtypeStruct((n_b, d), BF16),
        compiler_params=_params(1), name="conf_conv_sample")(st2, u_s, dw, dw_b, ln_g, ln_b)


def _short_conv_sample_kernel(n_hist, st_ref, u_ref, gate_ref, w_ref, o_ref):
    d = u_ref.shape[1]
    acc = w_ref[n_hist:n_hist + 1, :] * u_ref[...]
    for k in range(n_hist):
        acc = acc + w_ref[k:k + 1, :] * st_ref[:, k * d:(k + 1) * d]
    o_ref[...] = (gate_ref[...] * acc).astype(o_ref.dtype)


def short_conv_sample(state, u_s, gate_s, w_conv, *, tb):
    n_b, n_hist, d = state.shape
    st2 = state.reshape(n_b, n_hist * d)
    return pl.pallas_call(
        functools.partial(_short_conv_sample_kernel, n_hist),
        grid=(n_b // tb,),
        in_specs=[pl.BlockSpec((tb, n_hist * d), lambda i: (i, 0)),
                  pl.BlockSpec((tb, d), lambda i: (i, 0)),
                  pl.BlockSpec((tb, d), lambda i: (i, 0)),
                  pl.BlockSpec((n_hist + 1, d), lambda i: (0, 0))],
        out_specs=pl.BlockSpec((tb, d), lambda i: (i, 0)),
        out_shape=jax.ShapeDtypeStruct((n_b, d), BF16),
        compiler_params=_params(1), name="short_conv_sample")(st2, u_s, gate_s, w_conv)


ATT_TQ = 512
ATT_TK = 512


def _flash_kernel(seq, qn_ref, qp_ref, kn_ref, kp_ref, v_ref, o_ref):
    k_full = jnp.concatenate([kn_ref[...], kp_ref[...]], axis=-1)
    for qi in range(seq // ATT_TQ):
        q0 = qi * ATT_TQ
        q = jnp.concatenate([qn_ref[q0:q0 + ATT_TQ, :], qp_ref[q0:q0 + ATT_TQ, :]], axis=-1)
        m_run = jnp.full((ATT_TQ, 1), -jnp.inf, F32)
        l_run = jnp.zeros((ATT_TQ, 1), F32)
        acc = jnp.zeros((ATT_TQ, o_ref.shape[1]), F32)
        for ki in range((q0 + ATT_TQ) // ATT_TK):
            k0 = ki * ATT_TK
            s = lax.dot_general(q, k_full[k0:k0 + ATT_TK, :], (((1,), (1,)), ((), ())),
                                preferred_element_type=F32)
            if k0 + ATT_TK > q0:
                row = q0 + lax.broadcasted_iota(jnp.int32, s.shape, 0)
                col = k0 + lax.broadcasted_iota(jnp.int32, s.shape, 1)
                s = jnp.where(col <= row, s, -jnp.inf)
            m_new = jnp.maximum(m_run, jnp.max(s, axis=-1, keepdims=True))
            corr = jnp.exp(m_run - m_new)
            p = jnp.exp(s - m_new)
            l_run = l_run * corr + jnp.sum(p, axis=-1, keepdims=True)
            acc = acc * corr + jnp.dot(p.astype(BF16), v_ref[k0:k0 + ATT_TK, :],
                                       preferred_element_type=F32)
            m_run = m_new
        o_ref[q0:q0 + ATT_TQ, :] = (acc / l_run).astype(o_ref.dtype)


def flash_prompt(q_nope, q_pe, kv_up, kpe, *, batch, seq):
    hd = V_HEAD_DIM
    spec = lambda col: pl.BlockSpec((seq, hd), col)
    return pl.pallas_call(
        functools.partial(_flash_kernel, seq),
        grid=(batch, N_HEADS),
        in_specs=[spec(lambda b, h: (b, h)), spec(lambda b, h: (b, h)),
                  spec(lambda b, h: (b, h)), spec(lambda b, h: (b, 0)),
                  spec(lambda b, h: (b, h + N_HEADS))],
        out_specs=spec(lambda b, h: (b, h)),
        out_shape=jax.ShapeDtypeStruct((batch * seq, N_HEADS * hd), BF16),
        compiler_params=_params(2), name="flash_prompt")(q_nope, q_pe, kv_up, kpe, kv_up)


PAGES_PER_STEP = 8


def _per_head_nt_kernel(x_ref, w_ref, o_ref):
    o_ref[0] = lax.dot_general(x_ref[...], w_ref[...].astype(BF16), (((1,), (1,)), ((), ())),
                               preferred_element_type=F32).astype(o_ref.dtype)


def absorb_q(q_nope, w_uk2, *, row_block):
    r = w_uk2.shape[0]
    return pl.pallas_call(
        _per_head_nt_kernel, grid=(N_HEADS,),
        in_specs=[pl.BlockSpec((128, QK_NOPE_DIM), lambda h: (row_block, h)),
                  pl.BlockSpec((r, QK_NOPE_DIM), lambda h: (0, h))],
        out_specs=pl.BlockSpec((1, 128, r), lambda h: (h, 0, 0)),
        out_shape=jax.ShapeDtypeStruct((N_HEADS, 128, r), BF16),
        compiler_params=_params(1), name="absorb_q")(q_nope, w_uk2)


def _per_head_nn_kernel(x_ref, w_ref, o_ref):
    o_ref[...] = jnp.dot(x_ref[0], w_ref[...].astype(BF16),
                         preferred_element_type=F32).astype(o_ref.dtype)


def expand_o(o_lat_t, w_uv2):
    n_h, n_b, r = o_lat_t.shape
    return pl.pallas_call(
        _per_head_nn_kernel, grid=(n_h,),
        in_specs=[pl.BlockSpec((1, n_b, r), lambda h: (h, 0, 0)),
                  pl.BlockSpec((r, V_HEAD_DIM), lambda h: (0, h))],
        out_specs=pl.BlockSpec((n_b, V_HEAD_DIM), lambda h: (0, h)),
        out_shape=jax.ShapeDtypeStruct((n_b, n_h * V_HEAD_DIM), BF16),
        compiler_params=_params(1), name="expand_o")(o_lat_t, w_uv2)


def _decode_kernel(n_groups, pt_ref, ql_ref, qp_ref, cn_ref, kn_ref, *refs):
    g_pages = PAGES_PER_STEP
    ck_refs = refs[:g_pages]
    kp_refs = refs[g_pages:2 * g_pages]
    o_ref = refs[2 * g_pages]
    ck_buf, kp_buf, m_ref, l_ref, acc_ref = refs[2 * g_pages + 1:]
    g = pl.program_id(1)
    page = ck_refs[0].shape[1]

    @pl.when(g == 0)
    def _():
        m_ref[...] = jnp.full(m_ref.shape, -jnp.inf, F32)
        l_ref[...] = jnp.zeros(l_ref.shape, F32)
        acc_ref[...] = jnp.zeros(acc_ref.shape, F32)

    for i in range(g_pages):
        ck_buf[i * page:(i + 1) * page, :] = ck_refs[i][0].astype(BF16)
        kp_buf[i * page:(i + 1) * page, :] = kp_refs[i][0].astype(BF16)
    q = ql_ref[0]
    qp = qp_ref[0]
    ck = ck_buf[...]
    nt = (((1,), (1,)), ((), ()))
    s = (lax.dot_general(q, ck, nt, preferred_element_type=F32)
         + lax.dot_general(qp, kp_buf[...], nt, preferred_element_type=F32))
    m_old = m_ref[...]
    m_new = jnp.maximum(m_old, jnp.max(s, axis=-1, keepdims=True))
    corr = jnp.exp(m_old - m_new)
    p = jnp.exp(s - m_new)
    l_ref[...] = l_ref[...] * corr + jnp.sum(p, axis=-1, keepdims=True)
    acc_ref[...] = acc_ref[...] * corr + jnp.dot(p.astype(BF16), ck, preferred_element_type=F32)
    m_ref[...] = m_new

    @pl.when(g == n_groups - 1)
    def _():
        c_new = cn_ref[0].astype(BF16).astype(F32)
        k_new = kn_ref[0].astype(BF16).astype(F32)
        s_self = (jnp.sum(q.astype(F32) * c_new, axis=-1, keepdims=True)
                  + jnp.sum(qp.astype(F32) * k_new, axis=-1, keepdims=True))
        m_prev = m_ref[...]
        m_fin = jnp.maximum(m_prev, s_self)
        corr2 = jnp.exp(m_prev - m_fin)
        p_self = jnp.exp(s_self - m_fin)
        l_fin = l_ref[...] * corr2 + p_self
        acc_fin = acc_ref[...] * corr2 + p_self.astype(BF16).astype(F32) * c_new
        o_ref[0] = (acc_fin / l_fin).astype(o_ref.dtype)


def decode_attention(page_table, q_lat, q_pe, c_new, kpe_new, cache_ckv, cache_kpe):
    n_b, n_h, r = q_lat.shape
    p_dim = q_pe.shape[2]
    page = cache_ckv.shape[1]
    n_pages = page_table.shape[1]
    g_pages = PAGES_PER_STEP
    n_groups = n_pages // g_pages

    def page_spec(width, i):
        return pl.BlockSpec((1, page, width), lambda b, g, pt, i=i: (pt[b, g * g_pages + i], 0, 0))

    in_specs = [pl.BlockSpec((1, n_h, r), lambda b, g, pt: (b, 0, 0)),
                pl.BlockSpec((1, n_h, p_dim), lambda b, g, pt: (b, 0, 0)),
                pl.BlockSpec((1, 1, r), lambda b, g, pt: (b, 0, 0)),
                pl.BlockSpec((1, 1, p_dim), lambda b, g, pt: (b, 0, 0))]
    in_specs += [page_spec(r, i) for i in range(g_pages)]
    in_specs += [page_spec(p_dim, i) for i in range(g_pages)]
    grid_spec = pltpu.PrefetchScalarGridSpec(
        num_scalar_prefetch=1, grid=(n_b, n_groups), in_specs=in_specs,
        out_specs=pl.BlockSpec((1, n_h, r), lambda b, g, pt: (b, 0, 0)),
        scratch_shapes=[pltpu.VMEM((g_pages * page, r), BF16),
                        pltpu.VMEM((g_pages * page, p_dim), BF16),
                        pltpu.VMEM((n_h, 1), F32), pltpu.VMEM((n_h, 1), F32),
                        pltpu.VMEM((n_h, r), F32)])
    return pl.pallas_call(
        functools.partial(_decode_kernel, n_groups), grid_spec=grid_spec,
        out_shape=jax.ShapeDtypeStruct((n_b, n_h, r), BF16),
        compiler_params=_params(2), name="decode_attention")(
            page_table, q_lat, q_pe, c_new, kpe_new,
            *([cache_ckv] * g_pages), *([cache_kpe] * g_pages))


def _comb_identity(accs, ex):
    return (accs[0],)


def _comb_bias(accs, ex):
    return (accs[0] + ex[0],)


def _comb_glu(accs, ex):
    return ((accs[0] + ex[0]) * _sigmoid(accs[1] + ex[1]),)


def _comb_short_conv_in(accs, ex):
    return (accs[0], accs[1] * accs[2])


def _comb_rms(accs, ex):
    y = _rms(accs[0], ex[0])
    return (y, y)


def _tile_lanes(tab, width):
    reps = width // tab.shape[1]
    return tab if reps == 1 else jnp.concatenate([tab] * reps, axis=-1)


def _comb_rope(scale, accs, ex):
    cos_t = _tile_lanes(ex[0], accs[0].shape[1])
    sin_t = _tile_lanes(ex[1], accs[0].shape[1])
    return ((accs[0] * cos_t + accs[1] * sin_t) * scale,)


def _comb_scale(scale, accs, ex):
    return (accs[0] * scale,)


def _rot_cols(w):
    half = w.shape[-1] // 2
    return jnp.concatenate([-w[..., half:], w[..., :half]], axis=-1)


def _pad_lanes(w):
    pad = LANES - w.shape[-1]
    return jnp.concatenate([w, jnp.zeros(w.shape[:-1] + (pad,), w.dtype)], axis=-1)


def kernel(x_prompt, x_sample, state_conv_l0, cache_ckv_l1, cache_kpe_l1, state_sconv_l2, state_conv_l3, page_table, norm_g, ca_w_in, ca_b_in, ca_dw, ca_dw_b, ca_ln_g, ca_ln_b, ca_w_out, ca_b_out, mla_w_dq, mla_q_norm, mla_w_uq, mla_w_dkv, mla_kv_norm, mla_w_uk, mla_w_uv, mla_w_o, sc_w_in, sc_w_conv, sc_w_out, ffn_w_gu, ffn_w_down):
    b_p, s_p, d = x_prompt.shape
    b_s = x_sample.shape[0]
    m_p = b_p * s_p
    m = m_p + b_s
    depth = norm_g.shape[0]
    past_len = page_table.shape[1] * cache_ckv_l1.shape[1]
    conv_state_in = (state_conv_l0, state_conv_l3)

    tm = 1040
    tr = 320
    sample_block = m_p // b_s

    x = jnp.concatenate([x_prompt.reshape(m_p, d), x_sample.reshape(b_s, d)], axis=0)
    (h,) = _rowwise("rms_in", lambda r, v: (_rms(r[0], v[0]),), [x], [norm_g[0, 0][None]],
                    [(BF16, d)], tr=tr)

    half = QK_ROPE_DIM // 2
    inv = ROPE_THETA ** (-jnp.arange(half, dtype=F32) / half)
    pos = jnp.concatenate([jnp.tile(jnp.arange(s_p), b_p), past_len + jnp.zeros((b_s,), jnp.int32)])
    ang = pos.astype(F32)[:, None] * inv[None, :]
    cos_tab = _pad_lanes(jnp.concatenate([jnp.cos(ang)] * 2, axis=-1))
    sin_tab = _pad_lanes(jnp.concatenate([jnp.sin(ang)] * 2, axis=-1))

    outs_conv = []
    for i in range(depth):
        kind = i % N_MIXERS
        if kind == 0:
            j = i // N_MIXERS
            nb = d // 512
            (u,) = multi_mm("conf_in", h, [(ca_w_in[j], 0), (ca_w_in[j], nb)],
                            [(ca_b_in[j][None], "col", 0), (ca_b_in[j][None], "col", nb)],
                            _comb_glu, [F32], n_cols=d, tm=tm, tn=512)
            v_p = conf_conv_prompt(u, ca_dw[j], ca_dw_b[j][None], ca_ln_g[j][None], ca_ln_b[j][None],
                                   batch=b_p, seq=s_p, ts=128)
            u_s = u[m_p:]
            v_s = conf_conv_sample(conv_state_in[j], u_s, ca_dw[j], ca_dw_b[j][None],
                                   ca_ln_g[j][None], ca_ln_b[j][None], tb=16)
            v = jnp.concatenate([v_p, v_s], axis=0)
            (mix,) = multi_mm("conf_out", v, [(ca_w_out[j], 0)], [(ca_b_out[j][None], "col", 0)],
                              _comb_bias, [F32], n_cols=d, tm=tm, tn=1024)
            n_keep = ca_dw.shape[1] - 1
            conv_p = u[:m_p].reshape(b_p, s_p, d)[:, s_p - n_keep:]
            conv_s = jnp.concatenate([conv_state_in[j][:, 1:], u_s[:, None, :]], axis=1)
            outs_conv.append((conv_p, conv_s))
        elif kind == 1:
            r = KV_LORA_RANK
            qr = mla_w_dq.shape[1]
            w_down = jnp.concatenate([mla_w_dq, mla_w_dkv[:, :r]], axis=1)
            g_down = jnp.concatenate([mla_q_norm, mla_kv_norm])[None]
            lat_f32, lat_bf = multi_mm("mla_down", h, [(w_down, 0)], [(g_down, "col", 0)],
                                       _comb_rms, [F32, BF16], n_cols=qr + r, tm=tm, tn=512)
            w_kpe = mla_w_dkv[:, r:]
            (kpe,) = multi_mm("mla_kpe", h, [(_pad_lanes(w_kpe), 0), (_pad_lanes(_rot_cols(w_kpe)), 0)],
                              [(cos_tab, "row", 0), (sin_tab, "row", 0)],
                              functools.partial(_comb_rope, 1.0), [F32], n_cols=LANES, tm=tm, tn=LANES)
            cq = lat_bf[:, :qr]
            ckv_bf = lat_bf[:, qr:]
            w_uq3 = mla_w_uq.reshape(qr, N_HEADS, QK_NOPE_DIM + QK_ROPE_DIM)
            w_qn = w_uq3[:, :, :QK_NOPE_DIM].reshape(qr, N_HEADS * QK_NOPE_DIM)
            w_qp = w_uq3[:, :, QK_NOPE_DIM:]
            w_qp_a = _pad_lanes(w_qp).reshape(qr, N_HEADS * LANES)
            w_qp_b = _pad_lanes(_rot_cols(w_qp)).reshape(qr, N_HEADS * LANES)
            (q_nope,) = multi_mm("mla_qn", cq, [(w_qn, 0)], [],
                                 functools.partial(_comb_scale, ATTN_SCALE), [BF16],
                                 n_cols=N_HEADS * QK_NOPE_DIM, tm=tm, tn=1024)
            (q_pe,) = multi_mm("mla_qp", cq, [(w_qp_a, 0), (w_qp_b, 0)],
                               [(cos_tab, "row", 0), (sin_tab, "row", 0)],
                               functools.partial(_comb_rope, ATTN_SCALE), [BF16],
                               n_cols=N_HEADS * LANES, tm=tm, tn=1024)
            w_uk2 = mla_w_uk.reshape(r, N_HEADS * QK_NOPE_DIM)
            w_uv2 = mla_w_uv.reshape(r, N_HEADS * V_HEAD_DIM)
            w_up = jnp.concatenate([w_uk2, w_uv2], axis=1)
            (kv_up,) = multi_mm("mla_kv_up", ckv_bf, [(w_up, 0)], [], _comb_identity, [BF16],
                                n_cols=w_up.shape[1], tm=tm, tn=1024)
            kpe_bf = kpe.astype(BF16)
            att_p = flash_prompt(q_nope, q_pe, kv_up, kpe_bf, batch=b_p, seq=s_p)
            q_lat = absorb_q(q_nope, w_uk2, row_block=sample_block).transpose(1, 0, 2)
            q_pe_s = q_pe[m_p:].reshape(b_s, N_HEADS, LANES)[:, :, :QK_ROPE_DIM]
            ckv_s = lat_f32[m_p:, qr:].reshape(b_s, 1, r)
            kpe_s = kpe[m_p:, :QK_ROPE_DIM].reshape(b_s, 1, QK_ROPE_DIM)
            o_lat = decode_attention(page_table, q_lat, q_pe_s, ckv_s, kpe_s, cache_ckv_l1, cache_kpe_l1)
            att_s = expand_o(o_lat.transpose(1, 0, 2), w_uv2)
            att = jnp.concatenate([att_p, att_s], axis=0)
            (mix,) = multi_mm("mla_out", att, [(mla_w_o, 0)], [], _comb_identity, [F32],
                              n_cols=d, tm=tm, tn=1024)
            ckv_p = lat_f32[:m_p, qr:].reshape(b_p, s_p, r)
            kpe_p = kpe[:m_p, :QK_ROPE_DIM].reshape(b_p, s_p, QK_ROPE_DIM)
        else:
            nb = d // 512
            gate, u = multi_mm("sc_in", h, [(sc_w_in, 0), (sc_w_in, nb), (sc_w_in, 2 * nb)], [],
                               _comb_short_conv_in, [F32, F32], n_cols=d, tm=tm, tn=512)
            v_p = short_conv_prompt(u, gate, sc_w_conv, batch=b_p, seq=s_p, ts=256)
            u_s = u[m_p:]
            v_s = short_conv_sample(state_sconv_l2, u_s, gate[m_p:], sc_w_conv, tb=32)
            v = jnp.concatenate([v_p, v_s], axis=0)
            (mix,) = multi_mm("sc_out", v, [(sc_w_out, 0)], [], _comb_identity, [F32],
                              n_cols=d, tm=tm, tn=1024)
            n_keep = sc_w_conv.shape[0] - 1
            sc_p = u[:m_p].reshape(b_p, s_p, d)[:, s_p - n_keep:]
            sc_s = jnp.concatenate([state_sconv_l2[:, 1:], u_s[:, None, :]], axis=1)

        x, h = _rowwise("resid_norm", _resid_norm_fn, [x, mix], [norm_g[i, 1][None], norm_g[i, 2][None]],
                        [(F32, d), (BF16, d)], tr=tr)
        y = ffn(h, ffn_w_gu[i], ffn_w_down[i], tm=tm, tf=256)
        if i + 1 < depth:
            x, h = _rowwise("resid_norm", _resid_norm_fn, [x, y],
                            [norm_g[i, 3][None], norm_g[i + 1, 0][None]], [(F32, d), (BF16, d)], tr=tr)
        else:
            (x,) = _rowwise("resid", _resid_fn, [x, y], [norm_g[i, 3][None]], [(F32, d)], tr=tr)

    y_prompt = x[:m_p].reshape(b_p, s_p, d)
    y_sample = x[m_p:].reshape(b_s, 1, d)
    return (y_prompt, y_sample, outs_conv[0][0], outs_conv[0][1], ckv_p, kpe_p,
            ckv_s, kpe_s, sc_p, sc_s, outs_conv[1][0], outs_conv[1][1])
```

```python
import functools

import jax
import jax.numpy as jnp
from jax import lax
from jax.experimental import pallas as pl
from jax.experimental.pallas import tpu as pltpu

F32 = jnp.float32
BF16 = jnp.bfloat16

N_HEADS = 16
QK_NOPE_DIM = 128
QK_ROPE_DIM = 64
V_HEAD_DIM = 128
KV_LORA_RANK = 512
ROPE_THETA = 10000.0
ATTN_SCALE = (QK_NOPE_DIM + QK_ROPE_DIM) ** -0.5
NORM_EPS = 1e-6
N_MIXERS = 3
LANES = 128

VMEM_LIMIT_BYTES = 58 * 1024 * 1024

ROW_TILE = 1040
OUT_PROJ_TILE = 416
OUT_PROJ_SPLIT = 2
FFN_CHUNK = 256

_SINGLE = pl.Buffered(1)
_NT = (((1,), (1,)), ((), ()))


def _params(n_grid_axes):
    return pltpu.CompilerParams(
        dimension_semantics=("arbitrary",) * n_grid_axes,
        vmem_limit_bytes=VMEM_LIMIT_BYTES)


def _sigmoid(x):
    return 1.0 / (1.0 + jnp.exp(-x))


def _rms(x, g):
    return x * lax.rsqrt(jnp.mean(x * x, axis=-1, keepdims=True) + NORM_EPS) * g


def _layer_norm_silu(v, g, beta):
    mu = jnp.mean(v, axis=-1, keepdims=True)
    var = jnp.mean(jnp.square(v - mu), axis=-1, keepdims=True)
    y = (v - mu) * lax.rsqrt(var + NORM_EPS) * g + beta
    return y * _sigmoid(y)


def _stack_kernel(n_tail, xp_ref, xs_ref, g_ref, x_ref, h_ref):
    x_ref[...] = xp_ref[...]

    @pl.when(pl.program_id(0) == pl.num_programs(0) - 1)
    def _():
        x_ref[x_ref.shape[0] - n_tail:, :] = xs_ref[...]

    x = x_ref[...]
    h_ref[...] = _rms(x, g_ref[...]).astype(h_ref.dtype)


def stack_and_norm(xp, xs, gains, g_idx, *, tm):
    m_p, d = xp.shape
    m_s = xs.shape[0]
    m = m_p + m_s
    assert m % tm == 0 and m - m_p == m_s and tm - m_s >= 0 and (m_p % tm) == tm - m_s
    return pl.pallas_call(
        functools.partial(_stack_kernel, m_s),
        grid=(m // tm,),
        in_specs=[pl.BlockSpec((tm, d), lambda i: (i, 0)),
                  pl.BlockSpec((m_s, d), lambda i: (0, 0)),
                  pl.BlockSpec((None, 1, d), lambda i: (g_idx, 0, 0))],
        out_specs=[pl.BlockSpec((tm, d), lambda i: (i, 0)), pl.BlockSpec((tm, d), lambda i: (i, 0))],
        out_shape=[jax.ShapeDtypeStruct((m, d), F32), jax.ShapeDtypeStruct((m, d), BF16)],
        compiler_params=_params(1), name="stack_and_norm")(xp, xs, gains)


def _mm_kernel(n_w, n_ex, combine, x_ref, *refs):
    w_refs = refs[:n_w]
    ex_refs = refs[n_w:n_w + n_ex]
    out_refs = refs[n_w + n_ex:len(refs) - n_w]
    wb_refs = refs[len(refs) - n_w:]

    @pl.when(pl.program_id(1) == 0)
    def _():
        for w, wb in zip(w_refs, wb_refs):
            wb[...] = w[...].astype(BF16)

    x = x_ref[...]
    accs = [jnp.dot(x, wb[...], preferred_element_type=F32) for wb in wb_refs]
    outs = combine(accs, [e[...] for e in ex_refs])
    for o_ref, o in zip(out_refs, outs):
        o_ref[...] = o.astype(o_ref.dtype)


def multi_mm(name, x, ws, extras, combine, out_dtypes, *, n_cols, tm, tn):
    m, k = x.shape
    in_specs = [pl.BlockSpec((tm, k), lambda j, i: (i, 0))]
    args = [x]
    for w, layer, off in ws:
        in_specs.append(pl.BlockSpec((None, k, tn), lambda j, i, layer=layer, off=off: (layer, 0, j + off)))
        args.append(w)
    for arr, kind, layer, off in extras:
        if kind == "row":
            in_specs.append(pl.BlockSpec((tm, arr.shape[1]), lambda j, i: (i, 0)))
        else:
            in_specs.append(pl.BlockSpec((None, 1, tn), lambda j, i, layer=layer, off=off: (layer, 0, j + off)))
        args.append(arr)
    out_shape = [jax.ShapeDtypeStruct((m, n_cols), dt) for dt in out_dtypes]
    out_specs = [pl.BlockSpec((tm, tn), lambda j, i: (i, j)) for _ in out_dtypes]
    return pl.pallas_call(
        functools.partial(_mm_kernel, len(ws), len(extras), combine),
        grid=(n_cols // tn, m // tm), in_specs=in_specs, out_specs=out_specs, out_shape=out_shape,
        scratch_shapes=[pltpu.VMEM((k, tn), BF16) for _ in ws],
        compiler_params=_params(2), name=name)(*args)


def _out_proj_kernel(has_bias, v_ref, w_ref, *refs):
    if has_bias:
        b_ref, refs = refs[0], refs[1:]
    x_ref, g1_ref, g2_ref, xo_ref, h_ref, wb_ref = refs

    @pl.when(pl.program_id(0) == 0)
    def _():
        wb_ref[...] = w_ref[...].astype(BF16)

    rows = v_ref.shape[0] // OUT_PROJ_SPLIT
    for s in range(OUT_PROJ_SPLIT):
        sl = slice(s * rows, (s + 1) * rows)
        y = jnp.dot(v_ref[sl, :], wb_ref[...], preferred_element_type=F32)
        if has_bias:
            y = y + b_ref[...]
        x_new = x_ref[sl, :] + _rms(y, g1_ref[...])
        xo_ref[sl, :] = x_new
        h_ref[sl, :] = _rms(x_new, g2_ref[...]).astype(h_ref.dtype)


def out_proj(name, v, w, layer, bias, x, gains, g_post, g_next, *, tm):
    m, k = v.shape
    d = w.shape[2]
    in_specs = [pl.BlockSpec((tm, k), lambda i: (i, 0)),
                pl.BlockSpec((None, k, d), lambda i: (layer, 0, 0), pipeline_mode=_SINGLE)]
    args = [v, w]
    if bias is not None:
        in_specs.append(pl.BlockSpec((None, 1, d), lambda i: (layer, 0, 0)))
        args.append(bias)
    in_specs += [pl.BlockSpec((tm, d), lambda i: (i, 0)),
                 pl.BlockSpec((None, 1, d), lambda i: (g_post, 0, 0)),
                 pl.BlockSpec((None, 1, d), lambda i: (g_next, 0, 0))]
    args += [x, gains, gains]
    return pl.pallas_call(
        functools.partial(_out_proj_kernel, bias is not None),
        grid=(m // tm,), in_specs=in_specs,
        out_specs=[pl.BlockSpec((tm, d), lambda i: (i, 0)), pl.BlockSpec((tm, d), lambda i: (i, 0))],
        out_shape=[jax.ShapeDtypeStruct((m, d), F32), jax.ShapeDtypeStruct((m, d), BF16)],
        scratch_shapes=[pltpu.VMEM((k, d), BF16)],
        compiler_params=_params(1), name=name)(*args)


def _ffn_partial(h_ref, wg_ref, wu_ref, wd_ref):
    h = h_ref[...]
    g = jnp.dot(h, wg_ref[...].astype(BF16), preferred_element_type=F32)
    u = jnp.dot(h, wu_ref[...].astype(BF16), preferred_element_type=F32)
    act = (g * _sigmoid(g) * u).astype(BF16)
    return jnp.dot(act, wd_ref[...].astype(BF16), preferred_element_type=F32)


def _ffn_kernel(h_ref, x_ref, wg_ref, wu_ref, wd_ref, g1_ref, g2_ref, xo_ref, hn_ref):
    f = pl.program_id(1)

    @pl.when(f == 0)
    def _():
        xo_ref[...] = jnp.zeros(xo_ref.shape, F32)

    xo_ref[...] += _ffn_partial(h_ref, wg_ref, wu_ref, wd_ref)

    @pl.when(f == pl.num_programs(1) - 1)
    def _():
        x_new = x_ref[...] + _rms(xo_ref[...], g1_ref[...])
        xo_ref[...] = x_new
        hn_ref[...] = _rms(x_new, g2_ref[...]).astype(hn_ref.dtype)


def _ffn_last_kernel(n_tail, h_ref, x_ref, wg_ref, wu_ref, wd_ref, g1_ref, yp_ref, ys_ref):
    f = pl.program_id(1)

    @pl.when(f == 0)
    def _():
        yp_ref[...] = jnp.zeros(yp_ref.shape, F32)

    yp_ref[...] += _ffn_partial(h_ref, wg_ref, wu_ref, wd_ref)

    @pl.when(f == pl.num_programs(1) - 1)
    def _():
        yp_ref[...] = x_ref[...] + _rms(yp_ref[...], g1_ref[...])

        @pl.when(pl.program_id(0) == pl.num_programs(0) - 1)
        def _():
            ys_ref[...] = yp_ref[yp_ref.shape[0] - n_tail:, :]


def ffn(h, x, w_gu, w_down, layer, gains, g_post, g_next, *, tm, tf, n_tail):
    m, d = h.shape
    hidden = w_down.shape[1]
    n_f = hidden // tf
    in_specs = [pl.BlockSpec((tm, d), lambda i, f: (i, 0), pipeline_mode=_SINGLE),
                pl.BlockSpec((tm, d), lambda i, f: (i, 0), pipeline_mode=_SINGLE),
                pl.BlockSpec((None, d, tf), lambda i, f: (layer, 0, f)),
                pl.BlockSpec((None, d, tf), lambda i, f: (layer, 0, f + n_f)),
                pl.BlockSpec((None, tf, d), lambda i, f: (layer, f, 0)),
                pl.BlockSpec((None, 1, d), lambda i, f: (g_post, 0, 0))]
    args = [h, x, w_gu, w_gu, w_down, gains]
    row_spec = lambda: pl.BlockSpec((tm, d), lambda i, f: (i, 0), pipeline_mode=_SINGLE)
    if g_next is not None:
        in_specs.append(pl.BlockSpec((None, 1, d), lambda i, f: (g_next, 0, 0)))
        args.append(gains)
        return pl.pallas_call(
            _ffn_kernel, grid=(m // tm, n_f), in_specs=in_specs,
            out_specs=[row_spec(), row_spec()],
            out_shape=[jax.ShapeDtypeStruct((m, d), F32), jax.ShapeDtypeStruct((m, d), BF16)],
            compiler_params=_params(2), name="ffn")(*args)
    assert (m - n_tail) % tm == tm - n_tail
    return pl.pallas_call(
        functools.partial(_ffn_last_kernel, n_tail), grid=(m // tm, n_f), in_specs=in_specs,
        out_specs=[row_spec(), pl.BlockSpec((n_tail, d), lambda i, f: (0, 0))],
        out_shape=[jax.ShapeDtypeStruct((m - n_tail, d), F32), jax.ShapeDtypeStruct((n_tail, d), F32)],
        compiler_params=_params(2), name="ffn_last")(*args)


CONV_HALO = 32
CONV_ROWS = 32
CONV_COLS = 512
SUBLANES = 8


def _conf_conv_kernel(ts, n_taps, halo_ref, main_ref, w_ref, b_ref, g_ref, beta_ref, o_ref,
                      win_ref, sh_ref, conv_ref, w8_ref):
    t = pl.program_id(1)
    d = main_ref.shape[1]

    @pl.when((pl.program_id(0) == 0) & (t == 0))
    def _():
        for k in range(n_taps):
            w8_ref[k] = jnp.broadcast_to(w_ref[k:k + 1, :], (SUBLANES, d))

    win_ref[0:CONV_HALO, :] = jnp.where(t == 0, 0.0, halo_ref[...])
    win_ref[CONV_HALO:, :] = main_ref[...]
    sh_len = sh_ref.shape[1]
    for s in range(1, SUBLANES):
        for c0 in range(0, d, CONV_COLS):
            sh_ref[s - 1, :, c0:c0 + CONV_COLS] = win_ref[s:s + sh_len, c0:c0 + CONV_COLS]
    first = CONV_HALO - (n_taps - 1)

    def row_chunk(rc, carry):
        r0 = pl.multiple_of(rc * CONV_ROWS, CONV_ROWS)
        n_sub = CONV_ROWS // SUBLANES
        for c0 in range(0, d, CONV_COLS):
            accs = [None] * n_sub
            for k in range(n_taps):
                q, s = divmod(first + k, SUBLANES)
                rows = pl.ds(r0 + q * SUBLANES, CONV_ROWS)
                if s == 0:
                    src = win_ref[rows, c0:c0 + CONV_COLS]
                else:
                    src = sh_ref[s - 1, rows, c0:c0 + CONV_COLS]
                wk = w8_ref[k, :, c0:c0 + CONV_COLS]
                for rr in range(n_sub):
                    term = wk * src[rr * SUBLANES:(rr + 1) * SUBLANES, :]
                    accs[rr] = term if k == 0 else accs[rr] + term
            acc = jnp.concatenate(accs, axis=0)
            conv_ref[pl.ds(r0, CONV_ROWS), c0:c0 + CONV_COLS] = acc + b_ref[:, c0:c0 + CONV_COLS]
        return carry

    lax.fori_loop(0, ts // CONV_ROWS, row_chunk, 0)
    o_ref[...] = _layer_norm_silu(conv_ref[...], g_ref[...], beta_ref[...]).astype(o_ref.dtype)


def conf_conv_prompt(u, dw, dw_b, ln_g, ln_b, layer, *, batch, seq, ts):
    m, d = u.shape
    n_taps = dw.shape[1]
    n_t = seq // ts
    halo_per_tile = ts // CONV_HALO
    vec = lambda: pl.BlockSpec((None, 1, d), lambda b, t: (layer, 0, 0))
    return pl.pallas_call(
        functools.partial(_conf_conv_kernel, ts, n_taps),
        grid=(batch, n_t),
        in_specs=[
            pl.BlockSpec((CONV_HALO, d),
                         lambda b, t: (jnp.maximum((b * n_t + t) * halo_per_tile - 1, 0), 0)),
            pl.BlockSpec((ts, d), lambda b, t: (b * n_t + t, 0)),
            pl.BlockSpec((None, n_taps, d), lambda b, t: (layer, 0, 0)),
            vec(), vec(), vec()],
        out_specs=pl.BlockSpec((ts, d), lambda b, t: (b * n_t + t, 0)),
        out_shape=jax.ShapeDtypeStruct((m, d), BF16),
        scratch_shapes=[pltpu.VMEM((ts + CONV_HALO, d), F32),
                        pltpu.VMEM((SUBLANES - 1, ts + CONV_HALO - SUBLANES, d), F32),
                        pltpu.VMEM((ts, d), F32),
                        pltpu.VMEM((n_taps, SUBLANES, d), F32)],
        compiler_params=_params(2), name="conf_conv_prompt")(u, u, dw, dw_b, ln_g, ln_b)


SC_HALO = 8


def _short_conv_kernel(ts, n_taps, halo_ref, main_ref, gate_ref, w_ref, o_ref, win_ref):
    t = pl.program_id(1)
    win_ref[0:SC_HALO, :] = jnp.where(t == 0, 0.0, halo_ref[...])
    win_ref[SC_HALO:, :] = main_ref[...]
    first = SC_HALO - (n_taps - 1)
    acc = w_ref[0:1, :] * win_ref[first:first + ts, :]
    for k in range(1, n_taps):
        acc = acc + w_ref[k:k + 1, :] * win_ref[first + k:first + k + ts, :]
    o_ref[...] = (gate_ref[...] * acc).astype(o_ref.dtype)


def short_conv_prompt(u, gate, w_conv, *, batch, seq, ts):
    m, d = u.shape
    n_taps = w_conv.shape[0]
    n_t = seq // ts
    halo_per_tile = ts // SC_HALO
    return pl.pallas_call(
        functools.partial(_short_conv_kernel, ts, n_taps),
        grid=(batch, n_t),
        in_specs=[
            pl.BlockSpec((SC_HALO, d),
                         lambda b, t: (jnp.maximum((b * n_t + t) * halo_per_tile - 1, 0), 0)),
            pl.BlockSpec((ts, d), lambda b, t: (b * n_t + t, 0)),
            pl.BlockSpec((ts, d), lambda b, t: (b * n_t + t, 0)),
            pl.BlockSpec((n_taps, d), lambda b, t: (0, 0))],
        out_specs=pl.BlockSpec((ts, d), lambda b, t: (b * n_t + t, 0)),
        out_shape=jax.ShapeDtypeStruct((m, d), BF16),
        scratch_shapes=[pltpu.VMEM((ts + SC_HALO, d), F32)],
        compiler_params=_params(2), name="short_conv_prompt")(u, u, gate, w_conv)


def _conf_conv_sample_kernel(n_hist, buf_ref, st_ref, u_ref, w_ref, b_ref, g_ref, beta_ref, o_ref):
    del buf_ref
    acc = w_ref[n_hist:n_hist + 1, :] * u_ref[...] + b_ref[...]
    for k in range(n_hist):
        acc = acc + w_ref[k:k + 1, :] * st_ref[k]
    o_ref[...] = _layer_norm_silu(acc, g_ref[...], beta_ref[...]).astype(o_ref.dtype)


def conf_conv_sample(buf, state_t, u, dw, dw_b, ln_g, ln_b, layer, *, first_row, tb):
    n_hist, n_b, d = state_t.shape
    blk0 = first_row // tb
    vec = lambda: pl.BlockSpec((None, 1, d), lambda i: (layer, 0, 0))
    return pl.pallas_call(
        functools.partial(_conf_conv_sample_kernel, n_hist),
        grid=(n_b // tb,),
        in_specs=[pl.BlockSpec(memory_space=pl.ANY),
                  pl.BlockSpec((n_hist, tb, d), lambda i: (0, i, 0)),
                  pl.BlockSpec((tb, d), lambda i: (blk0 + i, 0)),
                  pl.BlockSpec((None, n_hist + 1, d), lambda i: (layer, 0, 0)),
                  vec(), vec(), vec()],
        out_specs=pl.BlockSpec((tb, d), lambda i: (blk0 + i, 0)),
        out_shape=jax.ShapeDtypeStruct(buf.shape, buf.dtype),
        input_output_aliases={0: 0},
        compiler_params=_params(1), name="conf_conv_sample")(buf, state_t, u, dw, dw_b, ln_g, ln_b)


def _short_conv_sample_kernel(n_hist, buf_ref, st_ref, u_ref, gate_ref, w_ref, o_ref):
    del buf_ref
    d = u_ref.shape[1]
    acc = w_ref[n_hist:n_hist + 1, :] * u_ref[...]
    for k in range(n_hist):
        acc = acc + w_ref[k:k + 1, :] * st_ref[:, k * d:(k + 1) * d]
    o_ref[...] = (gate_ref[...] * acc).astype(o_ref.dtype)


def short_conv_sample(buf, state, u, gate, w_conv, *, first_row, tb):
    n_b, n_hist, d = state.shape
    st2 = state.reshape(n_b, n_hist * d)
    blk0 = first_row // tb
    return pl.pallas_call(
        functools.partial(_short_conv_sample_kernel, n_hist),
        grid=(n_b // tb,),
        in_specs=[pl.BlockSpec(memory_space=pl.ANY),
                  pl.BlockSpec((tb, n_hist * d), lambda i: (i, 0)),
                  pl.BlockSpec((tb, d), lambda i: (blk0 + i, 0)),
                  pl.BlockSpec((tb, d), lambda i: (blk0 + i, 0)),
                  pl.BlockSpec((n_hist + 1, d), lambda i: (0, 0))],
        out_specs=pl.BlockSpec((tb, d), lambda i: (blk0 + i, 0)),
        out_shape=jax.ShapeDtypeStruct(buf.shape, buf.dtype),
        input_output_aliases={0: 0},
        compiler_params=_params(1), name="short_conv_sample")(buf, st2, u, gate, w_conv)


ATT_TQ = 512
ATT_TK = 512


def _flash_kernel(seq, qn_ref, qp_ref, kn_ref, kp_ref, v_ref, o_ref):
    k_full = jnp.concatenate([kn_ref[...], kp_ref[...]], axis=-1)
    for qi in range(seq // ATT_TQ):
        q0 = qi * ATT_TQ
        q = jnp.concatenate([qn_ref[q0:q0 + ATT_TQ, :], qp_ref[q0:q0 + ATT_TQ, :]], axis=-1)
        m_run = jnp.full((ATT_TQ, 1), -jnp.inf, F32)
        l_run = jnp.zeros((ATT_TQ, 1), F32)
        acc = jnp.zeros((ATT_TQ, o_ref.shape[1]), F32)
        for ki in range((q0 + ATT_TQ) // ATT_TK):
            k0 = ki * ATT_TK
            s = lax.dot_general(q, k_full[k0:k0 + ATT_TK, :], _NT, preferred_element_type=F32)
            if k0 + ATT_TK > q0:
                row = q0 + lax.broadcasted_iota(jnp.int32, s.shape, 0)
                col = k0 + lax.broadcasted_iota(jnp.int32, s.shape, 1)
                s = jnp.where(col <= row, s, -jnp.inf)
            m_new = jnp.maximum(m_run, jnp.max(s, axis=-1, keepdims=True))
            corr = jnp.exp(m_run - m_new)
            p = jnp.exp(s - m_new)
            l_run = l_run * corr + jnp.sum(p, axis=-1, keepdims=True)
            acc = acc * corr + jnp.dot(p.astype(BF16), v_ref[k0:k0 + ATT_TK, :],
                                       preferred_element_type=F32)
            m_run = m_new
        o_ref[q0:q0 + ATT_TQ, :] = (acc / l_run).astype(o_ref.dtype)


def flash_prompt(q_nope, q_pe, kv_up, kpe, *, batch, seq):
    hd = V_HEAD_DIM
    spec = lambda col: pl.BlockSpec((seq, hd), col)
    return pl.pallas_call(
        functools.partial(_flash_kernel, seq),
        grid=(batch, N_HEADS),
        in_specs=[spec(lambda b, h: (b, h)), spec(lambda b, h: (b, h)),
                  spec(lambda b, h: (b, h)), spec(lambda b, h: (b, 0)),
                  spec(lambda b, h: (b, h + N_HEADS))],
        out_specs=spec(lambda b, h: (b, h)),
        out_shape=jax.ShapeDtypeStruct((q_nope.shape[0], N_HEADS * hd), BF16),
        compiler_params=_params(2), name="flash_prompt")(q_nope, q_pe, kv_up, kpe, kv_up)


PAGES_PER_STEP = 8


def _per_head_nt_kernel(x_ref, w_ref, o_ref):
    o_ref[0] = lax.dot_general(x_ref[...], w_ref[...].astype(BF16), _NT,
                               preferred_element_type=F32).astype(o_ref.dtype)


def absorb_q(q_nope, w_uk2, *, row_block, n_rows):
    r = w_uk2.shape[0]
    return pl.pallas_call(
        _per_head_nt_kernel, grid=(N_HEADS,),
        in_specs=[pl.BlockSpec((n_rows, QK_NOPE_DIM), lambda h: (row_block, h)),
                  pl.BlockSpec((r, QK_NOPE_DIM), lambda h: (0, h))],
        out_specs=pl.BlockSpec((1, n_rows, r), lambda h: (h, 0, 0)),
        out_shape=jax.ShapeDtypeStruct((N_HEADS, n_rows, r), BF16),
        compiler_params=_params(1), name="absorb_q")(q_nope, w_uk2)


def _per_head_nn_kernel(buf_ref, x_ref, w_ref, o_ref):
    del buf_ref
    o_ref[...] = jnp.dot(x_ref[0], w_ref[...].astype(BF16),
                         preferred_element_type=F32).astype(o_ref.dtype)


def expand_o(buf, o_lat_t, w_uv2, *, row_block):
    n_h, n_b, r = o_lat_t.shape
    return pl.pallas_call(
        _per_head_nn_kernel, grid=(n_h,),
        in_specs=[pl.BlockSpec(memory_space=pl.ANY),
                  pl.BlockSpec((1, n_b, r), lambda h: (h, 0, 0)),
                  pl.BlockSpec((r, V_HEAD_DIM), lambda h: (0, h))],
        out_specs=pl.BlockSpec((n_b, V_HEAD_DIM), lambda h: (row_block, h)),
        out_shape=jax.ShapeDtypeStruct(buf.shape, buf.dtype),
        input_output_aliases={0: 0},
        compiler_params=_params(1), name="expand_o")(buf, o_lat_t, w_uv2)


def _decode_kernel(pt_ref, ql_ref, qp_ref, cn_ref, kn_ref, *refs):
    del pt_ref
    g_pages = PAGES_PER_STEP
    ck_refs = refs[:g_pages]
    kp_refs = refs[g_pages:2 * g_pages]
    o_ref = refs[2 * g_pages]
    ck_all, kp_buf, s_all = refs[2 * g_pages + 1:]
    g = pl.program_id(1)
    n_groups = pl.num_programs(1)
    page = ck_refs[0].shape[1]
    q = ql_ref[0]
    qp = qp_ref[0]

    for i in range(g_pages):
        ck_all[g, i * page:(i + 1) * page, :] = ck_refs[i][0].astype(BF16)
        kp_buf[:, i * page:(i + 1) * page] = kp_refs[i][0].astype(BF16)
    s_all[g] = (lax.dot_general(q, ck_all[g], _NT, preferred_element_type=F32)
                + jnp.dot(qp, kp_buf[...], preferred_element_type=F32))

    @pl.when(g == n_groups - 1)
    def _():
        n_g = s_all.shape[0]
        c_new = cn_ref[0].astype(BF16).astype(F32)
        k_new = kn_ref[0].astype(BF16).astype(F32)
        s_self = (jnp.sum(q.astype(F32) * c_new, axis=-1, keepdims=True)
                  + jnp.sum(qp.astype(F32) * k_new, axis=-1, keepdims=True))
        m_fin = s_self
        for j in range(n_g):
            m_fin = jnp.maximum(m_fin, jnp.max(s_all[j], axis=-1, keepdims=True))
        p_self = jnp.exp(s_self - m_fin)
        l_fin = p_self
        acc = p_self.astype(BF16).astype(F32) * c_new
        for j in range(n_g):
            p = jnp.exp(s_all[j] - m_fin)
            l_fin = l_fin + jnp.sum(p, axis=-1, keepdims=True)
            acc = acc + jnp.dot(p.astype(BF16), ck_all[j], preferred_element_type=F32)
        o_ref[0] = (acc / l_fin).astype(o_ref.dtype)


def decode_attention(page_table, q_lat, q_pe, c_new, kpe_new, cache_ckv, cache_kpe_t):
    n_b, n_h, r = q_lat.shape
    p_dim = q_pe.shape[2]
    page = cache_ckv.shape[1]
    n_pages = page_table.shape[1]
    g_pages = PAGES_PER_STEP
    n_groups = n_pages // g_pages

    def ck_spec(i):
        return pl.BlockSpec((1, page, r), lambda b, g, pt, i=i: (pt[b, g * g_pages + i], 0, 0))

    def kp_spec(i):
        return pl.BlockSpec((1, p_dim, page), lambda b, g, pt, i=i: (pt[b, g * g_pages + i], 0, 0))

    in_specs = [pl.BlockSpec((1, n_h, r), lambda b, g, pt: (b, 0, 0)),
                pl.BlockSpec((1, n_h, p_dim), lambda b, g, pt: (b, 0, 0)),
                pl.BlockSpec((1, 1, r), lambda b, g, pt: (b, 0, 0)),
                pl.BlockSpec((1, 1, p_dim), lambda b, g, pt: (b, 0, 0))]
    in_specs += [ck_spec(i) for i in range(g_pages)]
    in_specs += [kp_spec(i) for i in range(g_pages)]
    grid_spec = pltpu.PrefetchScalarGridSpec(
        num_scalar_prefetch=1, grid=(n_b, n_groups), in_specs=in_specs,
        out_specs=pl.BlockSpec((1, n_h, r), lambda b, g, pt: (b, 0, 0)),
        scratch_shapes=[pltpu.VMEM((n_groups, g_pages * page, r), BF16),
                        pltpu.VMEM((p_dim, g_pages * page), BF16),
                        pltpu.VMEM((n_groups, n_h, g_pages * page), F32)])
    return pl.pallas_call(
        _decode_kernel, grid_spec=grid_spec,
        out_shape=jax.ShapeDtypeStruct((n_b, n_h, r), BF16),
        compiler_params=_params(2), name="decode_attention")(
            page_table, q_lat, q_pe, c_new, kpe_new,
            *([cache_ckv] * g_pages), *([cache_kpe_t] * g_pages))


def _comb_identity(accs, ex):
    return (accs[0],)


def _comb_glu(accs, ex):
    return ((accs[0] + ex[0]) * _sigmoid(accs[1] + ex[1]),)


def _comb_short_conv_in(accs, ex):
    return (accs[0], accs[1] * accs[2])


def _comb_rms(accs, ex):
    y = _rms(accs[0], ex[0])
    return (y, y)


def _tile_lanes(tab, width):
    reps = width // tab.shape[1]
    return tab if reps == 1 else jnp.concatenate([tab] * reps, axis=-1)


def _comb_rope(scale, accs, ex):
    cos_t = _tile_lanes(ex[0], accs[0].shape[1])
    sin_t = _tile_lanes(ex[1], accs[0].shape[1])
    return ((accs[0] * cos_t + accs[1] * sin_t) * scale,)


def _comb_rope_f32_bf16(accs, ex):
    y = accs[0] * ex[0] + accs[1] * ex[1]
    return (y, y)


def _comb_scale(scale, accs, ex):
    return (accs[0] * scale,)


def _rot_cols(w):
    half = w.shape[-1] // 2
    return jnp.concatenate([-w[..., half:], w[..., :half]], axis=-1)


def _pad_lanes(w):
    pad = LANES - w.shape[-1]
    return jnp.concatenate([w, jnp.zeros(w.shape[:-1] + (pad,), w.dtype)], axis=-1)


def kernel(x_prompt, x_sample, state_conv_l0, cache_ckv_l1, cache_kpe_l1, state_sconv_l2, state_conv_l3, page_table, norm_g, ca_w_in, ca_b_in, ca_dw, ca_dw_b, ca_ln_g, ca_ln_b, ca_w_out, ca_b_out, mla_w_dq, mla_q_norm, mla_w_uq, mla_w_dkv, mla_kv_norm, mla_w_uk, mla_w_uv, mla_w_o, sc_w_in, sc_w_conv, sc_w_out, ffn_w_gu, ffn_w_down):
    b_p, s_p, d = x_prompt.shape
    b_s = x_sample.shape[0]
    m_p = b_p * s_p
    depth = norm_g.shape[0]
    past_len = page_table.shape[1] * cache_ckv_l1.shape[1]
    conv_state_in = (state_conv_l0, state_conv_l3)
    tm = ROW_TILE
    sample_block = m_p // b_s

    gains = norm_g.reshape(depth * 4, 1, d)
    ca_b_in3 = ca_b_in[:, None, :]
    ca_b_out3 = ca_b_out[:, None, :]
    ca_dw_b3 = ca_dw_b[:, None, :]
    ca_ln_g3 = ca_ln_g[:, None, :]
    ca_ln_b3 = ca_ln_b[:, None, :]

    x, h = stack_and_norm(x_prompt.reshape(m_p, d), x_sample.reshape(b_s, d), gains, 0, tm=tm)

    half = QK_ROPE_DIM // 2
    inv = ROPE_THETA ** (-jnp.arange(half, dtype=F32) / half)
    pos = jnp.concatenate([jnp.tile(jnp.arange(s_p), b_p), past_len + jnp.zeros((b_s,), jnp.int32)])
    ang = pos.astype(F32)[:, None] * inv[None, :]
    cos_tab = _pad_lanes(jnp.concatenate([jnp.cos(ang)] * 2, axis=-1))
    sin_tab = _pad_lanes(jnp.concatenate([jnp.sin(ang)] * 2, axis=-1))

    outs_conv = []
    for i in range(depth):
        kind = i % N_MIXERS
        g_post, g_ffn = 4 * i + 1, 4 * i + 2
        if kind == 0:
            j = i // N_MIXERS
            nb = d // 512
            (u,) = multi_mm("conf_in", h, [(ca_w_in, j, 0), (ca_w_in, j, nb)],
                            [(ca_b_in3, "col", j, 0), (ca_b_in3, "col", j, nb)],
                            _comb_glu, [F32], n_cols=d, tm=tm, tn=512)
            v = conf_conv_prompt(u, ca_dw, ca_dw_b3, ca_ln_g3, ca_ln_b3, j, batch=b_p, seq=s_p, ts=256)
            state_t = jnp.transpose(conv_state_in[j], (1, 0, 2))
            v = conf_conv_sample(v, state_t, u, ca_dw, ca_dw_b3, ca_ln_g3, ca_ln_b3, j,
                                 first_row=m_p, tb=16)
            x, h = out_proj("conf_out", v, ca_w_out, j, ca_b_out3, x, gains, g_post, g_ffn,
                            tm=OUT_PROJ_TILE)
            n_keep = ca_dw.shape[1] - 1
            u_s = u[m_p:]
            conv_p = u[:m_p].reshape(b_p, s_p, d)[:, s_p - n_keep:]
            conv_s = jnp.concatenate([conv_state_in[j][:, 1:], u_s[:, None, :]], axis=1)
            outs_conv.append((conv_p, conv_s))
        elif kind == 1:
            r = KV_LORA_RANK
            qr = mla_w_dq.shape[1]
            w_down = jnp.concatenate([mla_w_dq, mla_w_dkv[:, :r]], axis=1)[None]
            g_down = jnp.concatenate([mla_q_norm, mla_kv_norm])[None, None]
            lat_f32, lat_bf = multi_mm("mla_down", h, [(w_down, 0, 0)], [(g_down, "col", 0, 0)],
                                       _comb_rms, [F32, BF16], n_cols=qr + r, tm=tm, tn=512)
            w_kpe = mla_w_dkv[:, r:]
            kpe, kpe_bf = multi_mm("mla_kpe", h,
                                   [(_pad_lanes(w_kpe)[None], 0, 0), (_pad_lanes(_rot_cols(w_kpe))[None], 0, 0)],
                                   [(cos_tab, "row", None, None), (sin_tab, "row", None, None)],
                                   _comb_rope_f32_bf16, [F32, BF16], n_cols=LANES, tm=tm, tn=LANES)
            cq = lat_bf[:, :qr]
            ckv_bf = lat_bf[:, qr:]
            w_uq3 = mla_w_uq.reshape(qr, N_HEADS, QK_NOPE_DIM + QK_ROPE_DIM)
            w_qn = w_uq3[:, :, :QK_NOPE_DIM].reshape(1, qr, N_HEADS * QK_NOPE_DIM)
            w_qp = w_uq3[:, :, QK_NOPE_DIM:]
            w_qp_a = _pad_lanes(w_qp).reshape(1, qr, N_HEADS * LANES)
            w_qp_b = _pad_lanes(_rot_cols(w_qp)).reshape(1, qr, N_HEADS * LANES)
            (q_nope,) = multi_mm("mla_qn", cq, [(w_qn, 0, 0)], [],
                                 functools.partial(_comb_scale, ATTN_SCALE), [BF16],
                                 n_cols=N_HEADS * QK_NOPE_DIM, tm=tm, tn=1024)
            (q_pe,) = multi_mm("mla_qp", cq, [(w_qp_a, 0, 0), (w_qp_b, 0, 0)],
                               [(cos_tab, "row", None, None), (sin_tab, "row", None, None)],
                               functools.partial(_comb_rope, ATTN_SCALE), [BF16],
                               n_cols=N_HEADS * LANES, tm=tm, tn=1024)
            w_uk2 = mla_w_uk.reshape(r, N_HEADS * QK_NOPE_DIM)
            w_uv2 = mla_w_uv.reshape(r, N_HEADS * V_HEAD_DIM)
            w_up = jnp.concatenate([w_uk2, w_uv2], axis=1)[None]
            (kv_up,) = multi_mm("mla_kv_up", ckv_bf, [(w_up, 0, 0)], [], _comb_identity, [BF16],
                                n_cols=w_up.shape[2], tm=tm, tn=1024)
            att = flash_prompt(q_nope, q_pe, kv_up, kpe_bf, batch=b_p, seq=s_p)
            q_lat = absorb_q(q_nope, w_uk2, row_block=sample_block, n_rows=b_s).transpose(1, 0, 2)
            q_pe_s = q_pe[m_p:].reshape(b_s, N_HEADS, LANES)[:, :, :QK_ROPE_DIM]
            ckv_s = lat_f32[m_p:, qr:].reshape(b_s, 1, r)
            kpe_s = kpe[m_p:, :QK_ROPE_DIM].reshape(b_s, 1, QK_ROPE_DIM)
            cache_kpe_t = jnp.swapaxes(cache_kpe_l1, 1, 2)
            o_lat = decode_attention(page_table, q_lat, q_pe_s, ckv_s, kpe_s, cache_ckv_l1, cache_kpe_t)
            att = expand_o(att, o_lat.transpose(1, 0, 2), w_uv2, row_block=sample_block)
            x, h = out_proj("mla_out", att, mla_w_o[None], 0, None, x, gains, g_post, g_ffn,
                            tm=OUT_PROJ_TILE)
            ckv_p = lat_f32[:m_p, qr:].reshape(b_p, s_p, r)
            kpe_p = kpe[:m_p, :QK_ROPE_DIM].reshape(b_p, s_p, QK_ROPE_DIM)
        else:
            nb = d // 512
            sc_w_in3 = sc_w_in[None]
            gate, u = multi_mm("sc_in", h, [(sc_w_in3, 0, 0), (sc_w_in3, 0, nb), (sc_w_in3, 0, 2 * nb)], [],
                               _comb_short_conv_in, [F32, F32], n_cols=d, tm=tm, tn=512)
            v = short_conv_prompt(u, gate, sc_w_conv, batch=b_p, seq=s_p, ts=256)
            v = short_conv_sample(v, state_sconv_l2, u, gate, sc_w_conv, first_row=m_p, tb=32)
            x, h = out_proj("sc_out", v, sc_w_out[None], 0, None, x, gains, g_post, g_ffn,
                            tm=OUT_PROJ_TILE)
            n_keep = sc_w_conv.shape[0] - 1
            u_s = u[m_p:]
            sc_p = u[:m_p].reshape(b_p, s_p, d)[:, s_p - n_keep:]
            sc_s = jnp.concatenate([state_sconv_l2[:, 1:], u_s[:, None, :]], axis=1)

        g_next = 4 * (i + 1) if i + 1 < depth else None
        x, h = ffn(h, x, ffn_w_gu, ffn_w_down, i, gains, 4 * i + 3, g_next,
                   tm=tm, tf=FFN_CHUNK, n_tail=b_s)

    y_prompt = x.reshape(b_p, s_p, d)
    y_sample = h.reshape(b_s, 1, d)
    return (y_prompt, y_sample, outs_conv[0][0], outs_conv[0][1], ckv_p, kpe_p,
            ckv_s, kpe_s, sc_p, sc_s, outs_conv[1][0], outs_conv[1][1])
```

```python
import functools

import jax
import jax.numpy as jnp
from jax import lax
from jax.experimental import pallas as pl
from jax.experimental.pallas import tpu as pltpu

F32 = jnp.float32
BF16 = jnp.bfloat16

N_HEADS = 16
QK_NOPE_DIM = 128
QK_ROPE_DIM = 64
V_HEAD_DIM = 128
KV_LORA_RANK = 512
ROPE_THETA = 10000.0
ATTN_SCALE = (QK_NOPE_DIM + QK_ROPE_DIM) ** -0.5
NORM_EPS = 1e-6
N_MIXERS = 3
LANES = 128

VMEM_LIMIT_BYTES = 58 * 1024 * 1024

ROW_TILE = 1040
OUT_PROJ_TILE = 416
OUT_PROJ_SPLIT = 2
FFN_CHUNK = 256

_SINGLE = pl.Buffered(1)
_NT = (((1,), (1,)), ((), ()))


def _params(n_grid_axes):
    return pltpu.CompilerParams(
        dimension_semantics=("arbitrary",) * n_grid_axes,
        vmem_limit_bytes=VMEM_LIMIT_BYTES)


def _sigmoid(x):
    return 1.0 / (1.0 + jnp.exp(-x))


def _rms(x, g):
    return x * lax.rsqrt(jnp.mean(x * x, axis=-1, keepdims=True) + NORM_EPS) * g


def _layer_norm_silu(v, g, beta):
    mu = jnp.mean(v, axis=-1, keepdims=True)
    var = jnp.mean(jnp.square(v - mu), axis=-1, keepdims=True)
    y = (v - mu) * lax.rsqrt(var + NORM_EPS) * g + beta
    return y * _sigmoid(y)


def _stack_kernel(n_tail, xp_ref, xs_ref, g_ref, x_ref, h_ref):
    x_ref[...] = xp_ref[...]

    @pl.when(pl.program_id(0) == pl.num_programs(0) - 1)
    def _():
        x_ref[x_ref.shape[0] - n_tail:, :] = xs_ref[...]

    x = x_ref[...]
    h_ref[...] = _rms(x, g_ref[...]).astype(h_ref.dtype)


def stack_and_norm(xp, xs, gains, g_idx, *, tm):
    m_p, d = xp.shape
    m_s = xs.shape[0]
    m = m_p + m_s
    assert m % tm == 0 and m - m_p == m_s and tm - m_s >= 0 and (m_p % tm) == tm - m_s
    return pl.pallas_call(
        functools.partial(_stack_kernel, m_s),
        grid=(m // tm,),
        in_specs=[pl.BlockSpec((tm, d), lambda i: (i, 0)),
                  pl.BlockSpec((m_s, d), lambda i: (0, 0)),
                  pl.BlockSpec((None, 1, d), lambda i: (g_idx, 0, 0))],
        out_specs=[pl.BlockSpec((tm, d), lambda i: (i, 0)), pl.BlockSpec((tm, d), lambda i: (i, 0))],
        out_shape=[jax.ShapeDtypeStruct((m, d), F32), jax.ShapeDtypeStruct((m, d), BF16)],
        compiler_params=_params(1), name="stack_and_norm")(xp, xs, gains)


def _mm_kernel(n_w, n_ex, combine, x_ref, *refs):
    w_refs = refs[:n_w]
    ex_refs = refs[n_w:n_w + n_ex]
    out_refs = refs[n_w + n_ex:len(refs) - n_w]
    wb_refs = refs[len(refs) - n_w:]

    @pl.when(pl.program_id(1) == 0)
    def _():
        for w, wb in zip(w_refs, wb_refs):
            wb[...] = w[...].astype(BF16)

    x = x_ref[...]
    accs = [jnp.dot(x, wb[...], preferred_element_type=F32) for wb in wb_refs]
    outs = combine(accs, [e[...] for e in ex_refs])
    for o_ref, o in zip(out_refs, outs):
        o_ref[...] = o.astype(o_ref.dtype)


def multi_mm(name, x, ws, extras, combine, out_dtypes, *, n_cols, tm, tn):
    m, k = x.shape
    in_specs = [pl.BlockSpec((tm, k), lambda j, i: (i, 0))]
    args = [x]
    for w, layer, off in ws:
        in_specs.append(pl.BlockSpec((None, k, tn), lambda j, i, layer=layer, off=off: (layer, 0, j + off)))
        args.append(w)
    for arr, kind, layer, off in extras:
        if kind == "row":
            in_specs.append(pl.BlockSpec((tm, arr.shape[1]), lambda j, i: (i, 0)))
        else:
            in_specs.append(pl.BlockSpec((None, 1, tn), lambda j, i, layer=layer, off=off: (layer, 0, j + off)))
        args.append(arr)
    out_shape = [jax.ShapeDtypeStruct((m, n_cols), dt) for dt in out_dtypes]
    out_specs = [pl.BlockSpec((tm, tn), lambda j, i: (i, j)) for _ in out_dtypes]
    return pl.pallas_call(
        functools.partial(_mm_kernel, len(ws), len(extras), combine),
        grid=(n_cols // tn, m // tm), in_specs=in_specs, out_specs=out_specs, out_shape=out_shape,
        scratch_shapes=[pltpu.VMEM((k, tn), BF16) for _ in ws],
        compiler_params=_params(2), name=name)(*args)


def _out_proj_kernel(has_bias, v_ref, w_ref, *refs):
    if has_bias:
        b_ref, refs = refs[0], refs[1:]
    x_ref, g1_ref, g2_ref, xo_ref, h_ref, wb_ref = refs

    @pl.when(pl.program_id(0) == 0)
    def _():
        wb_ref[...] = w_ref[...].astype(BF16)

    rows = v_ref.shape[0] // OUT_PROJ_SPLIT
    for s in range(OUT_PROJ_SPLIT):
        sl = slice(s * rows, (s + 1) * rows)
        y = jnp.dot(v_ref[sl, :], wb_ref[...], preferred_element_type=F32)
        if has_bias:
            y = y + b_ref[...]
        x_new = x_ref[sl, :] + _rms(y, g1_ref[...])
        xo_ref[sl, :] = x_new
        h_ref[sl, :] = _rms(x_new, g2_ref[...]).astype(h_ref.dtype)


def out_proj(name, v, w, layer, bias, x, gains, g_post, g_next, *, tm):
    m, k = v.shape
    d = w.shape[2]
    in_specs = [pl.BlockSpec((tm, k), lambda i: (i, 0)),
                pl.BlockSpec((None, k, d), lambda i: (layer, 0, 0), pipeline_mode=_SINGLE)]
    args = [v, w]
    if bias is not None:
        in_specs.append(pl.BlockSpec((None, 1, d), lambda i: (layer, 0, 0)))
        args.append(bias)
    in_specs += [pl.BlockSpec((tm, d), lambda i: (i, 0)),
                 pl.BlockSpec((None, 1, d), lambda i: (g_post, 0, 0)),
                 pl.BlockSpec((None, 1, d), lambda i: (g_next, 0, 0))]
    args += [x, gains, gains]
    return pl.pallas_call(
        functools.partial(_out_proj_kernel, bias is not None),
        grid=(m // tm,), in_specs=in_specs,
        out_specs=[pl.BlockSpec((tm, d), lambda i: (i, 0)), pl.BlockSpec((tm, d), lambda i: (i, 0))],
        out_shape=[jax.ShapeDtypeStruct((m, d), F32), jax.ShapeDtypeStruct((m, d), BF16)],
        scratch_shapes=[pltpu.VMEM((k, d), BF16)],
        compiler_params=_params(1), name=name)(*args)


def _ffn_partial(h_ref, wg_ref, wu_ref, wd_ref):
    h = h_ref[...]
    g = jnp.dot(h, wg_ref[...].astype(BF16), preferred_element_type=F32)
    u = jnp.dot(h, wu_ref[...].astype(BF16), preferred_element_type=F32)
    act = (g * _sigmoid(g) * u).astype(BF16)
    return jnp.dot(act, wd_ref[...].astype(BF16), preferred_element_type=F32)


def _ffn_kernel(h_ref, x_ref, wg_ref, wu_ref, wd_ref, g1_ref, g2_ref, xo_ref, hn_ref):
    f = pl.program_id(1)

    @pl.when(f == 0)
    def _():
        xo_ref[...] = jnp.zeros(xo_ref.shape, F32)

    xo_ref[...] += _ffn_partial(h_ref, wg_ref, wu_ref, wd_ref)

    @pl.when(f == pl.num_programs(1) - 1)
    def _():
        x_new = x_ref[...] + _rms(xo_ref[...], g1_ref[...])
        xo_ref[...] = x_new
        hn_ref[...] = _rms(x_new, g2_ref[...]).astype(hn_ref.dtype)


def _ffn_last_kernel(n_tail, h_ref, x_ref, wg_ref, wu_ref, wd_ref, g1_ref, yp_ref, ys_ref):
    f = pl.program_id(1)

    @pl.when(f == 0)
    def _():
        yp_ref[...] = jnp.zeros(yp_ref.shape, F32)

    yp_ref[...] += _ffn_partial(h_ref, wg_ref, wu_ref, wd_ref)

    @pl.when(f == pl.num_programs(1) - 1)
    def _():
        yp_ref[...] = x_ref[...] + _rms(yp_ref[...], g1_ref[...])

        @pl.when(pl.program_id(0) == pl.num_programs(0) - 1)
        def _():
            ys_ref[...] = yp_ref[yp_ref.shape[0] - n_tail:, :]


def ffn(h, x, w_gu, w_down, layer, gains, g_post, g_next, *, tm, tf, n_tail):
    m, d = h.shape
    hidden = w_down.shape[1]
    n_f = hidden // tf
    in_specs = [pl.BlockSpec((tm, d), lambda i, f: (i, 0)),
                pl.BlockSpec((tm, d), lambda i, f: (i, 0)),
                pl.BlockSpec((None, d, tf), lambda i, f: (layer, 0, f)),
                pl.BlockSpec((None, d, tf), lambda i, f: (layer, 0, f + n_f)),
                pl.BlockSpec((None, tf, d), lambda i, f: (layer, f, 0)),
                pl.BlockSpec((None, 1, d), lambda i, f: (g_post, 0, 0))]
    args = [h, x, w_gu, w_gu, w_down, gains]
    row_spec = lambda: pl.BlockSpec((tm, d), lambda i, f: (i, 0), pipeline_mode=_SINGLE)
    if g_next is not None:
        in_specs.append(pl.BlockSpec((None, 1, d), lambda i, f: (g_next, 0, 0)))
        args.append(gains)
        return pl.pallas_call(
            _ffn_kernel, grid=(m // tm, n_f), in_specs=in_specs,
            out_specs=[row_spec(), row_spec()],
            out_shape=[jax.ShapeDtypeStruct((m, d), F32), jax.ShapeDtypeStruct((m, d), BF16)],
            compiler_params=_params(2), name="ffn")(*args)
    assert (m - n_tail) % tm == tm - n_tail
    return pl.pallas_call(
        functools.partial(_ffn_last_kernel, n_tail), grid=(m // tm, n_f), in_specs=in_specs,
        out_specs=[row_spec(), pl.BlockSpec((n_tail, d), lambda i, f: (0, 0))],
        out_shape=[jax.ShapeDtypeStruct((m - n_tail, d), F32), jax.ShapeDtypeStruct((n_tail, d), F32)],
        compiler_params=_params(2), name="ffn_last")(*args)


CONV_HALO = 32
CONV_ROWS = 32
CONV_COLS = 512
SUBLANES = 8


def _conf_conv_kernel(ts, n_taps, halo_ref, main_ref, w_ref, b_ref, g_ref, beta_ref, o_ref,
                      win_ref, sh_ref, conv_ref, w8_ref):
    t = pl.program_id(1)
    d = main_ref.shape[1]

    @pl.when((pl.program_id(0) == 0) & (t == 0))
    def _():
        for k in range(n_taps):
            w8_ref[k] = jnp.broadcast_to(w_ref[k:k + 1, :], (SUBLANES, d))

    win_ref[0:CONV_HALO, :] = jnp.where(t == 0, 0.0, halo_ref[...])
    win_ref[CONV_HALO:, :] = main_ref[...]
    sh_len = sh_ref.shape[1]
    for s in range(1, SUBLANES):
        for c0 in range(0, d, CONV_COLS):
            sh_ref[s - 1, :, c0:c0 + CONV_COLS] = win_ref[s:s + sh_len, c0:c0 + CONV_COLS]
    first = CONV_HALO - (n_taps - 1)

    def row_chunk(rc, carry):
        r0 = pl.multiple_of(rc * CONV_ROWS, CONV_ROWS)
        n_sub = CONV_ROWS // SUBLANES
        for c0 in range(0, d, CONV_COLS):
            accs = [None] * n_sub
            for k in range(n_taps):
                q, s = divmod(first + k, SUBLANES)
                rows = pl.ds(r0 + q * SUBLANES, CONV_ROWS)
                if s == 0:
                    src = win_ref[rows, c0:c0 + CONV_COLS]
                else:
                    src = sh_ref[s - 1, rows, c0:c0 + CONV_COLS]
                wk = w8_ref[k, :, c0:c0 + CONV_COLS]
                for rr in range(n_sub):
                    term = wk * src[rr * SUBLANES:(rr + 1) * SUBLANES, :]
                    accs[rr] = term if k == 0 else accs[rr] + term
            acc = jnp.concatenate(accs, axis=0)
            conv_ref[pl.ds(r0, CONV_ROWS), c0:c0 + CONV_COLS] = acc + b_ref[:, c0:c0 + CONV_COLS]
        return carry

    lax.fori_loop(0, ts // CONV_ROWS, row_chunk, 0)
    o_ref[...] = _layer_norm_silu(conv_ref[...], g_ref[...], beta_ref[...]).astype(o_ref.dtype)


def conf_conv_prompt(u, dw, dw_b, ln_g, ln_b, layer, *, batch, seq, ts):
    m, d = u.shape
    n_taps = dw.shape[1]
    n_t = seq // ts
    halo_per_tile = ts // CONV_HALO
    vec = lambda: pl.BlockSpec((None, 1, d), lambda b, t: (layer, 0, 0))
    return pl.pallas_call(
        functools.partial(_conf_conv_kernel, ts, n_taps),
        grid=(batch, n_t),
        in_specs=[
            pl.BlockSpec((CONV_HALO, d),
                         lambda b, t: (jnp.maximum((b * n_t + t) * halo_per_tile - 1, 0), 0)),
            pl.BlockSpec((ts, d), lambda b, t: (b * n_t + t, 0)),
            pl.BlockSpec((None, n_taps, d), lambda b, t: (layer, 0, 0)),
            vec(), vec(), vec()],
        out_specs=pl.BlockSpec((ts, d), lambda b, t: (b * n_t + t, 0)),
        out_shape=jax.ShapeDtypeStruct((m, d), BF16),
        scratch_shapes=[pltpu.VMEM((ts + CONV_HALO, d), F32),
                        pltpu.VMEM((SUBLANES - 1, ts + CONV_HALO - SUBLANES, d), F32),
                        pltpu.VMEM((ts, d), F32),
                        pltpu.VMEM((n_taps, SUBLANES, d), F32)],
        compiler_params=_params(2), name="conf_conv_prompt")(u, u, dw, dw_b, ln_g, ln_b)


SC_HALO = 8


def _short_conv_kernel(ts, n_taps, halo_ref, main_ref, gate_ref, w_ref, o_ref, win_ref):
    t = pl.program_id(1)
    win_ref[0:SC_HALO, :] = jnp.where(t == 0, 0.0, halo_ref[...])
    win_ref[SC_HALO:, :] = main_ref[...]
    first = SC_HALO - (n_taps - 1)
    acc = w_ref[0:1, :] * win_ref[first:first + ts, :]
    for k in range(1, n_taps):
        acc = acc + w_ref[k:k + 1, :] * win_ref[first + k:first + k + ts, :]
    o_ref[...] = (gate_ref[...] * acc).astype(o_ref.dtype)


def short_conv_prompt(u, gate, w_conv, *, batch, seq, ts):
    m, d = u.shape
    n_taps = w_conv.shape[0]
    n_t = seq // ts
    halo_per_tile = ts // SC_HALO
    return pl.pallas_call(
        functools.partial(_short_conv_kernel, ts, n_taps),
        grid=(batch, n_t),
        in_specs=[
            pl.BlockSpec((SC_HALO, d),
                         lambda b, t: (jnp.maximum((b * n_t + t) * halo_per_tile - 1, 0), 0)),
            pl.BlockSpec((ts, d), lambda b, t: (b * n_t + t, 0)),
            pl.BlockSpec((ts, d), lambda b, t: (b * n_t + t, 0)),
            pl.BlockSpec((n_taps, d), lambda b, t: (0, 0))],
        out_specs=pl.BlockSpec((ts, d), lambda b, t: (b * n_t + t, 0)),
        out_shape=jax.ShapeDtypeStruct((m, d), BF16),
        scratch_shapes=[pltpu.VMEM((ts + SC_HALO, d), F32)],
        compiler_params=_params(2), name="short_conv_prompt")(u, u, gate, w_conv)


def _conf_conv_sample_kernel(n_hist, buf_ref, st_ref, u_ref, w_ref, b_ref, g_ref, beta_ref, o_ref):
    del buf_ref
    acc = w_ref[n_hist:n_hist + 1, :] * u_ref[...] + b_ref[...]
    for k in range(n_hist):
        acc = acc + w_ref[k:k + 1, :] * st_ref[k]
    o_ref[...] = _layer_norm_silu(acc, g_ref[...], beta_ref[...]).astype(o_ref.dtype)


def conf_conv_sample(buf, state_t, u, dw, dw_b, ln_g, ln_b, layer, *, first_row, tb):
    n_hist, n_b, d = state_t.shape
    blk0 = first_row // tb
    vec = lambda: pl.BlockSpec((None, 1, d), lambda i: (layer, 0, 0))
    return pl.pallas_call(
        functools.partial(_conf_conv_sample_kernel, n_hist),
        grid=(n_b // tb,),
        in_specs=[pl.BlockSpec(memory_space=pl.ANY),
                  pl.BlockSpec((n_hist, tb, d), lambda i: (0, i, 0)),
                  pl.BlockSpec((tb, d), lambda i: (blk0 + i, 0)),
                  pl.BlockSpec((None, n_hist + 1, d), lambda i: (layer, 0, 0)),
                  vec(), vec(), vec()],
        out_specs=pl.BlockSpec((tb, d), lambda i: (blk0 + i, 0)),
        out_shape=jax.ShapeDtypeStruct(buf.shape, buf.dtype),
        input_output_aliases={0: 0},
        compiler_params=_params(1), name="conf_conv_sample")(buf, state_t, u, dw, dw_b, ln_g, ln_b)


def _short_conv_sample_kernel(n_hist, buf_ref, st_ref, u_ref, gate_ref, w_ref, o_ref):
    del buf_ref
    d = u_ref.shape[1]
    acc = w_ref[n_hist:n_hist + 1, :] * u_ref[...]
    for k in range(n_hist):
        acc = acc + w_ref[k:k + 1, :] * st_ref[:, k * d:(k + 1) * d]
    o_ref[...] = (gate_ref[...] * acc).astype(o_ref.dtype)


def short_conv_sample(buf, state, u, gate, w_conv, *, first_row, tb):
    n_b, n_hist, d = state.shape
    st2 = state.reshape(n_b, n_hist * d)
    blk0 = first_row // tb
    return pl.pallas_call(
        functools.partial(_short_conv_sample_kernel, n_hist),
        grid=(n_b // tb,),
        in_specs=[pl.BlockSpec(memory_space=pl.ANY),
                  pl.BlockSpec((tb, n_hist * d), lambda i: (i, 0)),
                  pl.BlockSpec((tb, d), lambda i: (blk0 + i, 0)),
                  pl.BlockSpec((tb, d), lambda i: (blk0 + i, 0)),
                  pl.BlockSpec((n_hist + 1, d), lambda i: (0, 0))],
        out_specs=pl.BlockSpec((tb, d), lambda i: (blk0 + i, 0)),
        out_shape=jax.ShapeDtypeStruct(buf.shape, buf.dtype),
        input_output_aliases={0: 0},
        compiler_params=_params(1), name="short_conv_sample")(buf, st2, u, gate, w_conv)


ATT_TQ = 512
ATT_TK = 512


def _flash_kernel(seq, qn_ref, qp_ref, kn_ref, kp_ref, v_ref, o_ref):
    k_full = jnp.concatenate([kn_ref[...], kp_ref[...]], axis=-1)
    for qi in range(seq // ATT_TQ):
        q0 = qi * ATT_TQ
        q = jnp.concatenate([qn_ref[q0:q0 + ATT_TQ, :], qp_ref[q0:q0 + ATT_TQ, :]], axis=-1)
        m_run = jnp.full((ATT_TQ, 1), -jnp.inf, F32)
        l_run = jnp.zeros((ATT_TQ, 1), F32)
        acc = jnp.zeros((ATT_TQ, o_ref.shape[1]), F32)
        for ki in range((q0 + ATT_TQ) // ATT_TK):
            k0 = ki * ATT_TK
            s = lax.dot_general(q, k_full[k0:k0 + ATT_TK, :], _NT, preferred_element_type=F32)
            if k0 + ATT_TK > q0:
                row = q0 + lax.broadcasted_iota(jnp.int32, s.shape, 0)
                col = k0 + lax.broadcasted_iota(jnp.int32, s.shape, 1)
                s = jnp.where(col <= row, s, -jnp.inf)
            m_new = jnp.maximum(m_run, jnp.max(s, axis=-1, keepdims=True))
            corr = jnp.exp(m_run - m_new)
            p = jnp.exp(s - m_new)
            l_run = l_run * corr + jnp.sum(p, axis=-1, keepdims=True)
            acc = acc * corr + jnp.dot(p.astype(BF16), v_ref[k0:k0 + ATT_TK, :],
                                       preferred_element_type=F32)
            m_run = m_new
        o_ref[q0:q0 + ATT_TQ, :] = (acc / l_run).astype(o_ref.dtype)


def flash_prompt(q_nope, q_pe, kv_up, kpe, *, batch, seq):
    hd = V_HEAD_DIM
    spec = lambda col: pl.BlockSpec((seq, hd), col)
    return pl.pallas_call(
        functools.partial(_flash_kernel, seq),
        grid=(batch, N_HEADS),
        in_specs=[spec(lambda b, h: (b, h)), spec(lambda b, h: (b, h)),
                  spec(lambda b, h: (b, h)), spec(lambda b, h: (b, 0)),
                  spec(lambda b, h: (b, h + N_HEADS))],
        out_specs=spec(lambda b, h: (b, h)),
        out_shape=jax.ShapeDtypeStruct((q_nope.shape[0], N_HEADS * hd), BF16),
        compiler_params=_params(2), name="flash_prompt")(q_nope, q_pe, kv_up, kpe, kv_up)


PAGES_PER_STEP = 8
PAGE_RING = 4


def _per_head_nt_kernel(x_ref, w_ref, o_ref):
    o_ref[0] = lax.dot_general(x_ref[...], w_ref[...].astype(BF16), _NT,
                               preferred_element_type=F32).astype(o_ref.dtype)


def absorb_q(q_nope, w_uk2, *, row_block, n_rows):
    r = w_uk2.shape[0]
    return pl.pallas_call(
        _per_head_nt_kernel, grid=(N_HEADS,),
        in_specs=[pl.BlockSpec((n_rows, QK_NOPE_DIM), lambda h: (row_block, h)),
                  pl.BlockSpec((r, QK_NOPE_DIM), lambda h: (0, h))],
        out_specs=pl.BlockSpec((1, n_rows, r), lambda h: (h, 0, 0)),
        out_shape=jax.ShapeDtypeStruct((N_HEADS, n_rows, r), BF16),
        compiler_params=_params(1), name="absorb_q")(q_nope, w_uk2)


def _per_head_nn_kernel(buf_ref, x_ref, w_ref, o_ref):
    del buf_ref
    o_ref[...] = jnp.dot(x_ref[0], w_ref[...].astype(BF16),
                         preferred_element_type=F32).astype(o_ref.dtype)


def expand_o(buf, o_lat_t, w_uv2, *, row_block):
    n_h, n_b, r = o_lat_t.shape
    return pl.pallas_call(
        _per_head_nn_kernel, grid=(n_h,),
        in_specs=[pl.BlockSpec(memory_space=pl.ANY),
                  pl.BlockSpec((1, n_b, r), lambda h: (h, 0, 0)),
                  pl.BlockSpec((r, V_HEAD_DIM), lambda h: (0, h))],
        out_specs=pl.BlockSpec((n_b, V_HEAD_DIM), lambda h: (row_block, h)),
        out_shape=jax.ShapeDtypeStruct(buf.shape, buf.dtype),
        input_output_aliases={0: 0},
        compiler_params=_params(1), name="expand_o")(buf, o_lat_t, w_uv2)


def _page_copies(pt_ref, ckv_hbm, kpt_hbm, ck_ring, kp_ring, sem_ck, sem_kp, chunk, n_groups):
    slot = chunk % PAGE_RING
    b = chunk // n_groups
    g = chunk % n_groups
    copies = []
    for i in range(PAGES_PER_STEP):
        pg = pt_ref[b, g * PAGES_PER_STEP + i]
        copies.append(pltpu.make_async_copy(ckv_hbm.at[pg], ck_ring.at[slot, i], sem_ck.at[slot]))
        copies.append(pltpu.make_async_copy(kpt_hbm.at[pg], kp_ring.at[slot, i], sem_kp.at[slot]))
    return copies


def _decode_kernel(pt_ref, ql_ref, qp_ref, cn_ref, kn_ref, ckv_hbm, kpt_hbm, o_ref,
                   ck_ring, kp_ring, sem_ck, sem_kp, ck_all, kp_buf, s_all):
    g = pl.program_id(1)
    n_groups = pl.num_programs(1)
    n_chunks = pl.num_programs(0) * n_groups
    chunk = pl.program_id(0) * n_groups + g
    page = ck_ring.shape[2]
    ring_args = (pt_ref, ckv_hbm, kpt_hbm, ck_ring, kp_ring, sem_ck, sem_kp)

    @pl.when(chunk == 0)
    def _():
        for c in range(PAGE_RING - 1):
            for cp in _page_copies(*ring_args, jnp.int32(c), n_groups):
                cp.start()

    @pl.when(chunk + (PAGE_RING - 1) < n_chunks)
    def _():
        for cp in _page_copies(*ring_args, chunk + (PAGE_RING - 1), n_groups):
            cp.start()

    for cp in _page_copies(*ring_args, chunk, n_groups):
        cp.wait()

    slot = chunk % PAGE_RING
    q = ql_ref[0]
    qp = qp_ref[0]
    for i in range(PAGES_PER_STEP):
        ck_all[g, i * page:(i + 1) * page, :] = ck_ring[slot, i].astype(BF16)
        kp_buf[:, i * page:(i + 1) * page] = kp_ring[slot, i].astype(BF16)
    s_all[g] = (lax.dot_general(q, ck_all[g], _NT, preferred_element_type=F32)
                + jnp.dot(qp, kp_buf[...], preferred_element_type=F32))

    @pl.when(g == n_groups - 1)
    def _():
        n_g = s_all.shape[0]
        c_new = cn_ref[0].astype(BF16).astype(F32)
        k_new = kn_ref[0].astype(BF16).astype(F32)
        s_self = (jnp.sum(q.astype(F32) * c_new, axis=-1, keepdims=True)
                  + jnp.sum(qp.astype(F32) * k_new, axis=-1, keepdims=True))
        m_fin = s_self
        for j in range(n_g):
            m_fin = jnp.maximum(m_fin, jnp.max(s_all[j], axis=-1, keepdims=True))
        p_self = jnp.exp(s_self - m_fin)
        l_fin = p_self
        acc = p_self.astype(BF16).astype(F32) * c_new
        for j in range(n_g):
            p = jnp.exp(s_all[j] - m_fin)
            l_fin = l_fin + jnp.sum(p, axis=-1, keepdims=True)
            acc = acc + jnp.dot(p.astype(BF16), ck_all[j], preferred_element_type=F32)
        o_ref[0] = (acc / l_fin).astype(o_ref.dtype)


def decode_attention(page_table, q_lat, q_pe, c_new, kpe_new, cache_ckv, cache_kpe_t):
    n_b, n_h, r = q_lat.shape
    p_dim = q_pe.shape[2]
    page = cache_ckv.shape[1]
    n_pages = page_table.shape[1]
    g_pages = PAGES_PER_STEP
    n_groups = n_pages // g_pages
    assert n_groups * g_pages == n_pages and n_b * n_groups >= PAGE_RING - 1

    in_specs = [pl.BlockSpec((1, n_h, r), lambda b, g, pt: (b, 0, 0)),
                pl.BlockSpec((1, n_h, p_dim), lambda b, g, pt: (b, 0, 0)),
                pl.BlockSpec((1, 1, r), lambda b, g, pt: (b, 0, 0)),
                pl.BlockSpec((1, 1, p_dim), lambda b, g, pt: (b, 0, 0)),
                pl.BlockSpec(memory_space=pl.ANY),
                pl.BlockSpec(memory_space=pl.ANY)]
    grid_spec = pltpu.PrefetchScalarGridSpec(
        num_scalar_prefetch=1, grid=(n_b, n_groups), in_specs=in_specs,
        out_specs=pl.BlockSpec((1, n_h, r), lambda b, g, pt: (b, 0, 0)),
        scratch_shapes=[pltpu.VMEM((PAGE_RING, g_pages, page, r), F32),
                        pltpu.VMEM((PAGE_RING, g_pages, p_dim, page), F32),
                        pltpu.SemaphoreType.DMA((PAGE_RING,)),
                        pltpu.SemaphoreType.DMA((PAGE_RING,)),
                        pltpu.VMEM((n_groups, g_pages * page, r), BF16),
                        pltpu.VMEM((p_dim, g_pages * page), BF16),
                        pltpu.VMEM((n_groups, n_h, g_pages * page), F32)])
    return pl.pallas_call(
        _decode_kernel, grid_spec=grid_spec,
        out_shape=jax.ShapeDtypeStruct((n_b, n_h, r), BF16),
        compiler_params=_params(2), name="decode_attention")(
            page_table, q_lat, q_pe, c_new, kpe_new, cache_ckv, cache_kpe_t)


def _comb_identity(accs, ex):
    return (accs[0],)


def _comb_glu(accs, ex):
    return ((accs[0] + ex[0]) * _sigmoid(accs[1] + ex[1]),)


def _comb_short_conv_in(accs, ex):
    return (accs[0], accs[1] * accs[2])


def _comb_rms(accs, ex):
    y = _rms(accs[0], ex[0])
    return (y, y)


def _tile_lanes(tab, width):
    reps = width // tab.shape[1]
    return tab if reps == 1 else jnp.concatenate([tab] * reps, axis=-1)


def _comb_rope(scale, accs, ex):
    cos_t = _tile_lanes(ex[0], accs[0].shape[1])
    sin_t = _tile_lanes(ex[1], accs[0].shape[1])
    return ((accs[0] * cos_t + accs[1] * sin_t) * scale,)


def _comb_rope_f32_bf16(accs, ex):
    y = accs[0] * ex[0] + accs[1] * ex[1]
    return (y, y)


def _comb_scale(scale, accs, ex):
    return (accs[0] * scale,)


def _rot_cols(w):
    half = w.shape[-1] // 2
    return jnp.concatenate([-w[..., half:], w[..., :half]], axis=-1)


def _last_rows(u, n_batch, seq, n_keep):
    return jnp.stack([u[(b + 1) * seq - n_keep:(b + 1) * seq] for b in range(n_batch)])


def _pad_lanes(w):
    pad = LANES - w.shape[-1]
    return jnp.concatenate([w, jnp.zeros(w.shape[:-1] + (pad,), w.dtype)], axis=-1)


def kernel(x_prompt, x_sample, state_conv_l0, cache_ckv_l1, cache_kpe_l1, state_sconv_l2, state_conv_l3, page_table, norm_g, ca_w_in, ca_b_in, ca_dw, ca_dw_b, ca_ln_g, ca_ln_b, ca_w_out, ca_b_out, mla_w_dq, mla_q_norm, mla_w_uq, mla_w_dkv, mla_kv_norm, mla_w_uk, mla_w_uv, mla_w_o, sc_w_in, sc_w_conv, sc_w_out, ffn_w_gu, ffn_w_down):
    b_p, s_p, d = x_prompt.shape
    b_s = x_sample.shape[0]
    m_p = b_p * s_p
    depth = norm_g.shape[0]
    past_len = page_table.shape[1] * cache_ckv_l1.shape[1]
    conv_state_in = (state_conv_l0, state_conv_l3)
    tm = ROW_TILE
    sample_block = m_p // b_s

    gains = norm_g.reshape(depth * 4, 1, d)
    ca_b_in3 = ca_b_in[:, None, :]
    ca_b_out3 = ca_b_out[:, None, :]
    ca_dw_b3 = ca_dw_b[:, None, :]
    ca_ln_g3 = ca_ln_g[:, None, :]
    ca_ln_b3 = ca_ln_b[:, None, :]

    x, h = stack_and_norm(x_prompt.reshape(m_p, d), x_sample.reshape(b_s, d), gains, 0, tm=tm)

    half = QK_ROPE_DIM // 2
    inv = ROPE_THETA ** (-jnp.arange(half, dtype=F32) / half)
    pos = jnp.concatenate([jnp.tile(jnp.arange(s_p), b_p), past_len + jnp.zeros((b_s,), jnp.int32)])
    ang = pos.astype(F32)[:, None] * inv[None, :]
    cos_tab = _pad_lanes(jnp.concatenate([jnp.cos(ang)] * 2, axis=-1))
    sin_tab = _pad_lanes(jnp.concatenate([jnp.sin(ang)] * 2, axis=-1))

    outs_conv = []
    for i in range(depth):
        kind = i % N_MIXERS
        g_post, g_ffn = 4 * i + 1, 4 * i + 2
        if kind == 0:
            j = i // N_MIXERS
            nb = d // 512
            (u,) = multi_mm("conf_in", h, [(ca_w_in, j, 0), (ca_w_in, j, nb)],
                            [(ca_b_in3, "col", j, 0), (ca_b_in3, "col", j, nb)],
                            _comb_glu, [F32], n_cols=d, tm=tm, tn=512)
            v = conf_conv_prompt(u, ca_dw, ca_dw_b3, ca_ln_g3, ca_ln_b3, j, batch=b_p, seq=s_p, ts=256)
            state_t = jnp.transpose(conv_state_in[j], (1, 0, 2))
            v = conf_conv_sample(v, state_t, u, ca_dw, ca_dw_b3, ca_ln_g3, ca_ln_b3, j,
                                 first_row=m_p, tb=16)
            x, h = out_proj("conf_out", v, ca_w_out, j, ca_b_out3, x, gains, g_post, g_ffn,
                            tm=OUT_PROJ_TILE)
            n_keep = ca_dw.shape[1] - 1
            u_s = u[m_p:]
            conv_p = _last_rows(u, b_p, s_p, n_keep)
            conv_s = jnp.concatenate([conv_state_in[j][:, 1:], u_s[:, None, :]], axis=1)
            outs_conv.append((conv_p, conv_s))
        elif kind == 1:
            r = KV_LORA_RANK
            qr = mla_w_dq.shape[1]
            w_down = jnp.concatenate([mla_w_dq, mla_w_dkv[:, :r]], axis=1)[None]
            g_down = jnp.concatenate([mla_q_norm, mla_kv_norm])[None, None]
            lat_f32, lat_bf = multi_mm("mla_down", h, [(w_down, 0, 0)], [(g_down, "col", 0, 0)],
                                       _comb_rms, [F32, BF16], n_cols=qr + r, tm=tm, tn=512)
            w_kpe = mla_w_dkv[:, r:]
            kpe, kpe_bf = multi_mm("mla_kpe", h,
                                   [(_pad_lanes(w_kpe)[None], 0, 0), (_pad_lanes(_rot_cols(w_kpe))[None], 0, 0)],
                                   [(cos_tab, "row", None, None), (sin_tab, "row", None, None)],
                                   _comb_rope_f32_bf16, [F32, BF16], n_cols=LANES, tm=tm, tn=LANES)
            cq = lat_bf[:, :qr]
            ckv_bf = lat_bf[:, qr:]
            w_uq3 = mla_w_uq.reshape(qr, N_HEADS, QK_NOPE_DIM + QK_ROPE_DIM)
            w_qn = w_uq3[:, :, :QK_NOPE_DIM].reshape(1, qr, N_HEADS * QK_NOPE_DIM)
            w_qp = w_uq3[:, :, QK_NOPE_DIM:]
            w_qp_a = _pad_lanes(w_qp).reshape(1, qr, N_HEADS * LANES)
            w_qp_b = _pad_lanes(_rot_cols(w_qp)).reshape(1, qr, N_HEADS * LANES)
            (q_nope,) = multi_mm("mla_qn", cq, [(w_qn, 0, 0)], [],
                                 functools.partial(_comb_scale, ATTN_SCALE), [BF16],
                                 n_cols=N_HEADS * QK_NOPE_DIM, tm=tm, tn=1024)
            (q_pe,) = multi_mm("mla_qp", cq, [(w_qp_a, 0, 0), (w_qp_b, 0, 0)],
                               [(cos_tab, "row", None, None), (sin_tab, "row", None, None)],
                               functools.partial(_comb_rope, ATTN_SCALE), [BF16],
                               n_cols=N_HEADS * LANES, tm=tm, tn=1024)
            w_uk2 = mla_w_uk.reshape(r, N_HEADS * QK_NOPE_DIM)
            w_uv2 = mla_w_uv.reshape(r, N_HEADS * V_HEAD_DIM)
            w_up = jnp.concatenate([w_uk2, w_uv2], axis=1)[None]
            (kv_up,) = multi_mm("mla_kv_up", ckv_bf, [(w_up, 0, 0)], [], _comb_identity, [BF16],
                                n_cols=w_up.shape[2], tm=tm, tn=1024)
            att = flash_prompt(q_nope, q_pe, kv_up, kpe_bf, batch=b_p, seq=s_p)
            q_lat = absorb_q(q_nope, w_uk2, row_block=sample_block, n_rows=b_s).transpose(1, 0, 2)
            q_pe_s = q_pe[m_p:].reshape(b_s, N_HEADS, LANES)[:, :, :QK_ROPE_DIM]
            ckv_s = lat_f32[m_p:, qr:].reshape(b_s, 1, r)
            kpe_s = kpe[m_p:, :QK_ROPE_DIM].reshape(b_s, 1, QK_ROPE_DIM)
            cache_kpe_t = jnp.swapaxes(cache_kpe_l1, 1, 2)
            o_lat = decode_attention(page_table, q_lat, q_pe_s, ckv_s, kpe_s, cache_ckv_l1, cache_kpe_t)
            att = expand_o(att, o_lat.transpose(1, 0, 2), w_uv2, row_block=sample_block)
            x, h = out_proj("mla_out", att, mla_w_o[None], 0, None, x, gains, g_post, g_ffn,
                            tm=OUT_PROJ_TILE)
            ckv_p = lat_f32[:m_p, qr:].reshape(b_p, s_p, r)
            kpe_p = kpe[:m_p, :QK_ROPE_DIM].reshape(b_p, s_p, QK_ROPE_DIM)
        else:
            nb = d // 512
            sc_w_in3 = sc_w_in[None]
            gate, u = multi_mm("sc_in", h, [(sc_w_in3, 0, 0), (sc_w_in3, 0, nb), (sc_w_in3, 0, 2 * nb)], [],
                               _comb_short_conv_in, [F32, F32], n_cols=d, tm=tm, tn=512)
            v = short_conv_prompt(u, gate, sc_w_conv, batch=b_p, seq=s_p, ts=256)
            v = short_conv_sample(v, state_sconv_l2, u, gate, sc_w_conv, first_row=m_p, tb=32)
            x, h = out_proj("sc_out", v, sc_w_out[None], 0, None, x, gains, g_post, g_ffn,
                            tm=OUT_PROJ_TILE)
            n_keep = sc_w_conv.shape[0] - 1
            u_s = u[m_p:]
            sc_p = _last_rows(u, b_p, s_p, n_keep)
            sc_s = jnp.concatenate([state_sconv_l2[:, 1:], u_s[:, None, :]], axis=1)

        g_next = 4 * (i + 1) if i + 1 < depth else None
        x, h = ffn(h, x, ffn_w_gu, ffn_w_down, i, gains, 4 * i + 3, g_next,
                   tm=tm, tf=FFN_CHUNK, n_tail=b_s)

    y_prompt = x.reshape(b_p, s_p, d)
    y_sample = h.reshape(b_s, 1, d)
    return (y_prompt, y_sample, outs_conv[0][0], outs_conv[0][1], ckv_p, kpe_p,
            ckv_s, kpe_s, sc_p, sc_s, outs_conv[1][0], outs_conv[1][1])
```

```python
import functools

import jax
import jax.numpy as jnp
from jax import lax
from jax.experimental import pallas as pl
from jax.experimental.pallas import tpu as pltpu

F32 = jnp.float32
BF16 = jnp.bfloat16

N_HEADS = 16
QK_NOPE_DIM = 128
QK_ROPE_DIM = 64
V_HEAD_DIM = 128
KV_LORA_RANK = 512
ROPE_THETA = 10000.0
ATTN_SCALE = (QK_NOPE_DIM + QK_ROPE_DIM) ** -0.5
LOG2_E = 1.4426950408889634
Q_SCALE = ATTN_SCALE * LOG2_E
NORM_EPS = 1e-6
N_MIXERS = 3
LANES = 128

VMEM_LIMIT_BYTES = 58 * 1024 * 1024

ROW_TILE = 1040
OUT_PROJ_TILE = 416
OUT_PROJ_SPLIT = 2
FFN_CHUNK = 256

_SINGLE = pl.Buffered(1)
_NT = (((1,), (1,)), ((), ()))
_TN = (((0,), (0,)), ((), ()))


def _params(n_grid_axes):
    return pltpu.CompilerParams(
        dimension_semantics=("arbitrary",) * n_grid_axes,
        vmem_limit_bytes=VMEM_LIMIT_BYTES)


def _sigmoid(x):
    return 1.0 / (1.0 + jnp.exp(-x))


def _rms(x, g):
    return x * lax.rsqrt(jnp.mean(x * x, axis=-1, keepdims=True) + NORM_EPS) * g


def _layer_norm_silu(v, g, beta):
    mu = jnp.mean(v, axis=-1, keepdims=True)
    var = jnp.mean(jnp.square(v - mu), axis=-1, keepdims=True)
    y = (v - mu) * lax.rsqrt(var + NORM_EPS) * g + beta
    return y * _sigmoid(y)


def _stack_kernel(n_tail, xp_ref, xs_ref, g_ref, x_ref, h_ref):
    x_ref[...] = xp_ref[...]

    @pl.when(pl.program_id(0) == pl.num_programs(0) - 1)
    def _():
        x_ref[x_ref.shape[0] - n_tail:, :] = xs_ref[...]

    x = x_ref[...]
    h_ref[...] = _rms(x, g_ref[...]).astype(h_ref.dtype)


def stack_and_norm(xp, xs, gains, g_idx, *, tm):
    m_p, d = xp.shape
    m_s = xs.shape[0]
    m = m_p + m_s
    assert m % tm == 0 and m - m_p == m_s and tm - m_s >= 0 and (m_p % tm) == tm - m_s
    return pl.pallas_call(
        functools.partial(_stack_kernel, m_s),
        grid=(m // tm,),
        in_specs=[pl.BlockSpec((tm, d), lambda i: (i, 0)),
                  pl.BlockSpec((m_s, d), lambda i: (0, 0)),
                  pl.BlockSpec((None, 1, d), lambda i: (g_idx, 0, 0))],
        out_specs=[pl.BlockSpec((tm, d), lambda i: (i, 0)), pl.BlockSpec((tm, d), lambda i: (i, 0))],
        out_shape=[jax.ShapeDtypeStruct((m, d), F32), jax.ShapeDtypeStruct((m, d), BF16)],
        compiler_params=_params(1), name="stack_and_norm")(xp, xs, gains)


def _mm_kernel(n_w, n_ex, combine, x_ref, *refs):
    w_refs = refs[:n_w]
    ex_refs = refs[n_w:n_w + n_ex]
    out_refs = refs[n_w + n_ex:len(refs) - n_w]
    wb_refs = refs[len(refs) - n_w:]

    @pl.when(pl.program_id(1) == 0)
    def _():
        for w, wb in zip(w_refs, wb_refs):
            wb[...] = w[...].astype(BF16)

    x = x_ref[...]
    accs = [jnp.dot(x, wb[...], preferred_element_type=F32) for wb in wb_refs]
    outs = combine(accs, [e[...] for e in ex_refs])
    for o_ref, o in zip(out_refs, outs):
        o_ref[...] = o.astype(o_ref.dtype)


def multi_mm(name, x, ws, extras, combine, out_dtypes, *, n_cols, tm, tn):
    m, k = x.shape
    in_specs = [pl.BlockSpec((tm, k), lambda j, i: (i, 0))]
    args = [x]
    for w, layer, off in ws:
        in_specs.append(pl.BlockSpec((None, k, tn), lambda j, i, layer=layer, off=off: (layer, 0, j + off)))
        args.append(w)
    for arr, kind, layer, off in extras:
        if kind == "row":
            in_specs.append(pl.BlockSpec((tm, arr.shape[1]), lambda j, i: (i, 0)))
        else:
            in_specs.append(pl.BlockSpec((None, 1, tn), lambda j, i, layer=layer, off=off: (layer, 0, j + off)))
        args.append(arr)
    out_shape = [jax.ShapeDtypeStruct((m, n_cols), dt) for dt in out_dtypes]
    out_specs = [pl.BlockSpec((tm, tn), lambda j, i: (i, j)) for _ in out_dtypes]
    return pl.pallas_call(
        functools.partial(_mm_kernel, len(ws), len(extras), combine),
        grid=(n_cols // tn, m // tm), in_specs=in_specs, out_specs=out_specs, out_shape=out_shape,
        scratch_shapes=[pltpu.VMEM((k, tn), BF16) for _ in ws],
        compiler_params=_params(2), name=name)(*args)


def _out_proj_kernel(has_bias, v_ref, w_ref, *refs):
    if has_bias:
        b_ref, refs = refs[0], refs[1:]
    x_ref, g1_ref, g2_ref, xo_ref, h_ref, wb_ref = refs

    @pl.when(pl.program_id(0) == 0)
    def _():
        wb_ref[...] = w_ref[...].astype(BF16)

    rows = v_ref.shape[0] // OUT_PROJ_SPLIT
    for s in range(OUT_PROJ_SPLIT):
        sl = slice(s * rows, (s + 1) * rows)
        y = jnp.dot(v_ref[sl, :], wb_ref[...], preferred_element_type=F32)
        if has_bias:
            y = y + b_ref[...]
        x_new = x_ref[sl, :] + _rms(y, g1_ref[...])
        xo_ref[sl, :] = x_new
        h_ref[sl, :] = _rms(x_new, g2_ref[...]).astype(h_ref.dtype)


def out_proj(name, v, w, layer, bias, x, gains, g_post, g_next, *, tm):
    m, k = v.shape
    d = w.shape[2]
    in_specs = [pl.BlockSpec((tm, k), lambda i: (i, 0)),
                pl.BlockSpec((None, k, d), lambda i: (layer, 0, 0), pipeline_mode=_SINGLE)]
    args = [v, w]
    if bias is not None:
        in_specs.append(pl.BlockSpec((None, 1, d), lambda i: (layer, 0, 0)))
        args.append(bias)
    in_specs += [pl.BlockSpec((tm, d), lambda i: (i, 0)),
                 pl.BlockSpec((None, 1, d), lambda i: (g_post, 0, 0)),
                 pl.BlockSpec((None, 1, d), lambda i: (g_next, 0, 0))]
    args += [x, gains, gains]
    return pl.pallas_call(
        functools.partial(_out_proj_kernel, bias is not None),
        grid=(m // tm,), in_specs=in_specs,
        out_specs=[pl.BlockSpec((tm, d), lambda i: (i, 0)), pl.BlockSpec((tm, d), lambda i: (i, 0))],
        out_shape=[jax.ShapeDtypeStruct((m, d), F32), jax.ShapeDtypeStruct((m, d), BF16)],
        scratch_shapes=[pltpu.VMEM((k, d), BF16)],
        compiler_params=_params(1), name=name)(*args)


def _ffn_partial(h_ref, wg_ref, wu_ref, wd_ref):
    h = h_ref[...]
    g = jnp.dot(h, wg_ref[...].astype(BF16), preferred_element_type=F32)
    u = jnp.dot(h, wu_ref[...].astype(BF16), preferred_element_type=F32)
    act = (g * _sigmoid(g) * u).astype(BF16)
    return jnp.dot(act, wd_ref[...].astype(BF16), preferred_element_type=F32)


def _ffn_kernel(h_ref, x_ref, wg_ref, wu_ref, wd_ref, g1_ref, g2_ref, xo_ref, hn_ref):
    f = pl.program_id(1)

    @pl.when(f == 0)
    def _():
        xo_ref[...] = jnp.zeros(xo_ref.shape, F32)

    xo_ref[...] += _ffn_partial(h_ref, wg_ref, wu_ref, wd_ref)

    @pl.when(f == pl.num_programs(1) - 1)
    def _():
        x_new = x_ref[...] + _rms(xo_ref[...], g1_ref[...])
        xo_ref[...] = x_new
        hn_ref[...] = _rms(x_new, g2_ref[...]).astype(hn_ref.dtype)


def _ffn_last_kernel(n_tail, h_ref, x_ref, wg_ref, wu_ref, wd_ref, g1_ref, yp_ref, ys_ref):
    f = pl.program_id(1)

    @pl.when(f == 0)
    def _():
        yp_ref[...] = jnp.zeros(yp_ref.shape, F32)

    yp_ref[...] += _ffn_partial(h_ref, wg_ref, wu_ref, wd_ref)

    @pl.when(f == pl.num_programs(1) - 1)
    def _():
        yp_ref[...] = x_ref[...] + _rms(yp_ref[...], g1_ref[...])

        @pl.when(pl.program_id(0) == pl.num_programs(0) - 1)
        def _():
            ys_ref[...] = yp_ref[yp_ref.shape[0] - n_tail:, :]


def ffn(h, x, w_gu, w_down, layer, gains, g_post, g_next, *, tm, tf, n_tail):
    m, d = h.shape
    hidden = w_down.shape[1]
    n_f = hidden // tf
    in_specs = [pl.BlockSpec((tm, d), lambda i, f: (i, 0)),
                pl.BlockSpec((tm, d), lambda i, f: (i, 0)),
                pl.BlockSpec((None, d, tf), lambda i, f: (layer, 0, f)),
                pl.BlockSpec((None, d, tf), lambda i, f: (layer, 0, f + n_f)),
                pl.BlockSpec((None, tf, d), lambda i, f: (layer, f, 0)),
                pl.BlockSpec((None, 1, d), lambda i, f: (g_post, 0, 0))]
    args = [h, x, w_gu, w_gu, w_down, gains]
    row_spec = lambda: pl.BlockSpec((tm, d), lambda i, f: (i, 0), pipeline_mode=_SINGLE)
    if g_next is not None:
        in_specs.append(pl.BlockSpec((None, 1, d), lambda i, f: (g_next, 0, 0)))
        args.append(gains)
        return pl.pallas_call(
            _ffn_kernel, grid=(m // tm, n_f), in_specs=in_specs,
            out_specs=[row_spec(), row_spec()],
            out_shape=[jax.ShapeDtypeStruct((m, d), F32), jax.ShapeDtypeStruct((m, d), BF16)],
            compiler_params=_params(2), name="ffn")(*args)
    assert (m - n_tail) % tm == tm - n_tail
    return pl.pallas_call(
        functools.partial(_ffn_last_kernel, n_tail), grid=(m // tm, n_f), in_specs=in_specs,
        out_specs=[row_spec(), pl.BlockSpec((n_tail, d), lambda i, f: (0, 0))],
        out_shape=[jax.ShapeDtypeStruct((m - n_tail, d), F32), jax.ShapeDtypeStruct((n_tail, d), F32)],
        compiler_params=_params(2), name="ffn_last")(*args)


CONV_HALO = 32
CONV_ROWS = 128
SUBLANES = 8


def _conf_conv_kernel(ts, n_taps, halo_ref, main_ref, w_ref, b_ref, g_ref, beta_ref, o_ref,
                      win_ref, conv_ref, w8_ref):
    t = pl.program_id(1)
    d = main_ref.shape[1]
    n_c = d // LANES
    n_sub = CONV_ROWS // SUBLANES
    col_slices = [slice(c * LANES, (c + 1) * LANES) for c in range(n_c)]

    @pl.when((pl.program_id(0) == 0) & (t == 0))
    def _():
        for k in range(n_taps):
            for c, cols in enumerate(col_slices):
                w8_ref[k, c] = jnp.broadcast_to(w_ref[k:k + 1, cols], (SUBLANES, LANES))

    for c, cols in enumerate(col_slices):
        win_ref[c, 0:CONV_HALO, :] = jnp.where(t == 0, 0.0, halo_ref[:, cols])
        win_ref[c, CONV_HALO:, :] = main_ref[:, cols]
    first = CONV_HALO - (n_taps - 1)

    def column(c, carry):
        def row_chunk(rc, carry2):
            r0 = pl.multiple_of(rc * CONV_ROWS, CONV_ROWS)
            accs = [None] * n_sub
            for k in range(n_taps):
                src = win_ref[c, pl.ds(r0 + first + k, CONV_ROWS), :]
                wk = w8_ref[k, c]
                for rr in range(n_sub):
                    term = wk * src[rr * SUBLANES:(rr + 1) * SUBLANES, :]
                    accs[rr] = term if k == 0 else accs[rr] + term
            conv_ref[c, pl.ds(r0, CONV_ROWS), :] = jnp.concatenate(accs, axis=0)
            return carry2

        return lax.fori_loop(0, ts // CONV_ROWS, row_chunk, carry)

    lax.fori_loop(0, n_c, column, 0)

    vs = [conv_ref[c] + b_ref[:, cols] for c, cols in enumerate(col_slices)]
    mu = jnp.sum(functools.reduce(jnp.add, vs), axis=-1, keepdims=True) / d
    sq = functools.reduce(jnp.add, [jnp.square(v - mu) for v in vs])
    rstd = lax.rsqrt(jnp.sum(sq, axis=-1, keepdims=True) / d + NORM_EPS)
    for v, cols in zip(vs, col_slices):
        y = (v - mu) * rstd * g_ref[:, cols] + beta_ref[:, cols]
        o_ref[:, cols] = (y * _sigmoid(y)).astype(o_ref.dtype)


def conf_conv_prompt(u, dw, dw_b, ln_g, ln_b, layer, *, batch, seq, ts):
    m, d = u.shape
    n_taps = dw.shape[1]
    n_t = seq // ts
    halo_per_tile = ts // CONV_HALO
    vec = lambda: pl.BlockSpec((None, 1, d), lambda b, t: (layer, 0, 0))
    return pl.pallas_call(
        functools.partial(_conf_conv_kernel, ts, n_taps),
        grid=(batch, n_t),
        in_specs=[
            pl.BlockSpec((CONV_HALO, d),
                         lambda b, t: (jnp.maximum((b * n_t + t) * halo_per_tile - 1, 0), 0)),
            pl.BlockSpec((ts, d), lambda b, t: (b * n_t + t, 0)),
            pl.BlockSpec((None, n_taps, d), lambda b, t: (layer, 0, 0)),
            vec(), vec(), vec()],
        out_specs=pl.BlockSpec((ts, d), lambda b, t: (b * n_t + t, 0)),
        out_shape=jax.ShapeDtypeStruct((m, d), BF16),
        scratch_shapes=[pltpu.VMEM((d // LANES, ts + CONV_HALO, LANES), F32),
                        pltpu.VMEM((d // LANES, ts, LANES), F32),
                        pltpu.VMEM((n_taps, d // LANES, SUBLANES, LANES), F32)],
        compiler_params=_params(2), name="conf_conv_prompt")(u, u, dw, dw_b, ln_g, ln_b)


SC_HALO = 8


def _short_conv_kernel(ts, n_taps, halo_ref, main_ref, gate_ref, w_ref, o_ref, win_ref):
    t = pl.program_id(1)
    d = main_ref.shape[1]
    first = SC_HALO - (n_taps - 1)
    for c in range(d // LANES):
        cols = slice(c * LANES, (c + 1) * LANES)
        win_ref[c, 0:SC_HALO, :] = jnp.where(t == 0, 0.0, halo_ref[:, cols])
        win_ref[c, SC_HALO:, :] = main_ref[:, cols]
    for c in range(d // LANES):
        cols = slice(c * LANES, (c + 1) * LANES)
        acc = w_ref[0:1, cols] * win_ref[c, first:first + ts, :]
        for k in range(1, n_taps):
            acc = acc + w_ref[k:k + 1, cols] * win_ref[c, first + k:first + k + ts, :]
        o_ref[:, cols] = (gate_ref[:, cols] * acc).astype(o_ref.dtype)


def short_conv_prompt(u, gate, w_conv, *, batch, seq, ts):
    m, d = u.shape
    n_taps = w_conv.shape[0]
    n_t = seq // ts
    halo_per_tile = ts // SC_HALO
    return pl.pallas_call(
        functools.partial(_short_conv_kernel, ts, n_taps),
        grid=(batch, n_t),
        in_specs=[
            pl.BlockSpec((SC_HALO, d),
                         lambda b, t: (jnp.maximum((b * n_t + t) * halo_per_tile - 1, 0), 0)),
            pl.BlockSpec((ts, d), lambda b, t: (b * n_t + t, 0)),
            pl.BlockSpec((ts, d), lambda b, t: (b * n_t + t, 0)),
            pl.BlockSpec((n_taps, d), lambda b, t: (0, 0))],
        out_specs=pl.BlockSpec((ts, d), lambda b, t: (b * n_t + t, 0)),
        out_shape=jax.ShapeDtypeStruct((m, d), BF16),
        scratch_shapes=[pltpu.VMEM((d // LANES, ts + SC_HALO, LANES), F32)],
        compiler_params=_params(2), name="short_conv_prompt")(u, u, gate, w_conv)


def _conf_conv_sample_kernel(n_hist, buf_ref, st_ref, u_ref, w_ref, b_ref, g_ref, beta_ref, o_ref):
    del buf_ref
    acc = w_ref[n_hist:n_hist + 1, :] * u_ref[...] + b_ref[...]
    for k in range(n_hist):
        acc = acc + w_ref[k:k + 1, :] * st_ref[k]
    o_ref[...] = _layer_norm_silu(acc, g_ref[...], beta_ref[...]).astype(o_ref.dtype)


def conf_conv_sample(buf, state_t, u, dw, dw_b, ln_g, ln_b, layer, *, first_row, tb):
    n_hist, n_b, d = state_t.shape
    blk0 = first_row // tb
    vec = lambda: pl.BlockSpec((None, 1, d), lambda i: (layer, 0, 0))
    return pl.pallas_call(
        functools.partial(_conf_conv_sample_kernel, n_hist),
        grid=(n_b // tb,),
        in_specs=[pl.BlockSpec(memory_space=pl.ANY),
                  pl.BlockSpec((n_hist, tb, d), lambda i: (0, i, 0)),
                  pl.BlockSpec((tb, d), lambda i: (blk0 + i, 0)),
                  pl.BlockSpec((None, n_hist + 1, d), lambda i: (layer, 0, 0)),
                  vec(), vec(), vec()],
        out_specs=pl.BlockSpec((tb, d), lambda i: (blk0 + i, 0)),
        out_shape=jax.ShapeDtypeStruct(buf.shape, buf.dtype),
        input_output_aliases={0: 0},
        compiler_params=_params(1), name="conf_conv_sample")(buf, state_t, u, dw, dw_b, ln_g, ln_b)


def _short_conv_sample_kernel(n_hist, buf_ref, st_ref, u_ref, gate_ref, w_ref, o_ref):
    del buf_ref
    d = u_ref.shape[1]
    acc = w_ref[n_hist:n_hist + 1, :] * u_ref[...]
    for k in range(n_hist):
        acc = acc + w_ref[k:k + 1, :] * st_ref[:, k * d:(k + 1) * d]
    o_ref[...] = (gate_ref[...] * acc).astype(o_ref.dtype)


def short_conv_sample(buf, state, u, gate, w_conv, *, first_row, tb):
    n_b, n_hist, d = state.shape
    st2 = state.reshape(n_b, n_hist * d)
    blk0 = first_row // tb
    return pl.pallas_call(
        functools.partial(_short_conv_sample_kernel, n_hist),
        grid=(n_b // tb,),
        in_specs=[pl.BlockSpec(memory_space=pl.ANY),
                  pl.BlockSpec((tb, n_hist * d), lambda i: (i, 0)),
                  pl.BlockSpec((tb, d), lambda i: (blk0 + i, 0)),
                  pl.BlockSpec((tb, d), lambda i: (blk0 + i, 0)),
                  pl.BlockSpec((n_hist + 1, d), lambda i: (0, 0))],
        out_specs=pl.BlockSpec((tb, d), lambda i: (blk0 + i, 0)),
        out_shape=jax.ShapeDtypeStruct(buf.shape, buf.dtype),
        input_output_aliases={0: 0},
        compiler_params=_params(1), name="short_conv_sample")(buf, st2, u, gate, w_conv)


ATT_TQ = 512
ATT_TK = 512


def _flash_kernel(seq, qn_ref, qp_ref, kn_ref, kp_ref, v_ref, o_ref):
    k_full = jnp.concatenate([kn_ref[...], kp_ref[...]], axis=-1)
    n_q = seq // ATT_TQ
    qs = [jnp.concatenate([qn_ref[i * ATT_TQ:(i + 1) * ATT_TQ, :],
                           qp_ref[i * ATT_TQ:(i + 1) * ATT_TQ, :]], axis=-1) for i in range(n_q)]
    m_run = [jnp.full((1, ATT_TQ), -jnp.inf, F32) for _ in range(n_q)]
    l_run = [jnp.zeros((1, ATT_TQ), F32) for _ in range(n_q)]
    acc = [jnp.zeros((o_ref.shape[1], ATT_TQ), F32) for _ in range(n_q)]
    pairs = [(ki, qi) for ki in range(seq // ATT_TK) for qi in range(n_q)
             if (qi + 1) * ATT_TQ > ki * ATT_TK]

    def scores_t(ki, qi):
        k0 = ki * ATT_TK
        st = lax.dot_general(k_full[k0:k0 + ATT_TK, :], qs[qi], _NT, preferred_element_type=F32)
        if k0 + ATT_TK > qi * ATT_TQ:
            key = k0 + lax.broadcasted_iota(jnp.int32, st.shape, 0)
            qpos = qi * ATT_TQ + lax.broadcasted_iota(jnp.int32, st.shape, 1)
            st = jnp.where(key <= qpos, st, -jnp.inf)
        return st

    st_next = scores_t(*pairs[0])
    for t, (ki, qi) in enumerate(pairs):
        st = st_next
        if t + 1 < len(pairs):
            st_next = scores_t(*pairs[t + 1])
        m_new = jnp.maximum(m_run[qi], jnp.max(st, axis=0, keepdims=True))
        corr = jnp.exp2(m_run[qi] - m_new)
        p = jnp.exp2(st - m_new)
        l_run[qi] = l_run[qi] * corr + jnp.sum(p, axis=0, keepdims=True)
        m_run[qi] = m_new
        acc[qi] = acc[qi] * corr + lax.dot_general(
            v_ref[ki * ATT_TK:(ki + 1) * ATT_TK, :], p.astype(BF16), _TN, preferred_element_type=F32)
    for qi in range(n_q):
        o_ref[qi * ATT_TQ:(qi + 1) * ATT_TQ, :] = (acc[qi] / l_run[qi]).T.astype(o_ref.dtype)


def flash_prompt(q_nope, q_pe, kv_up, kpe, *, batch, seq):
    hd = V_HEAD_DIM
    spec = lambda col: pl.BlockSpec((seq, hd), col)
    return pl.pallas_call(
        functools.partial(_flash_kernel, seq),
        grid=(batch, N_HEADS),
        in_specs=[spec(lambda b, h: (b, h)), spec(lambda b, h: (b, h)),
                  spec(lambda b, h: (b, h)), spec(lambda b, h: (b, 0)),
                  spec(lambda b, h: (b, h + N_HEADS))],
        out_specs=spec(lambda b, h: (b, h)),
        out_shape=jax.ShapeDtypeStruct((q_nope.shape[0], N_HEADS * hd), BF16),
        compiler_params=_params(2), name="flash_prompt")(q_nope, q_pe, kv_up, kpe, kv_up)


PAGES_PER_STEP = 16
PAGE_RING = 3


def _per_head_nt_kernel(x_ref, w_ref, o_ref):
    o_ref[0] = lax.dot_general(x_ref[...], w_ref[...].astype(BF16), _NT,
                               preferred_element_type=F32).astype(o_ref.dtype)


def absorb_q(q_nope, w_uk2, *, row_block, n_rows):
    r = w_uk2.shape[0]
    return pl.pallas_call(
        _per_head_nt_kernel, grid=(N_HEADS,),
        in_specs=[pl.BlockSpec((n_rows, QK_NOPE_DIM), lambda h: (row_block, h)),
                  pl.BlockSpec((r, QK_NOPE_DIM), lambda h: (0, h))],
        out_specs=pl.BlockSpec((1, n_rows, r), lambda h: (h, 0, 0)),
        out_shape=jax.ShapeDtypeStruct((N_HEADS, n_rows, r), BF16),
        compiler_params=_params(1), name="absorb_q")(q_nope, w_uk2)


def _per_head_nn_kernel(buf_ref, x_ref, w_ref, o_ref):
    del buf_ref
    o_ref[...] = jnp.dot(x_ref[0], w_ref[...].astype(BF16),
                         preferred_element_type=F32).astype(o_ref.dtype)


def expand_o(buf, o_lat_t, w_uv2, *, row_block):
    n_h, n_b, r = o_lat_t.shape
    return pl.pallas_call(
        _per_head_nn_kernel, grid=(n_h,),
        in_specs=[pl.BlockSpec(memory_space=pl.ANY),
                  pl.BlockSpec((1, n_b, r), lambda h: (h, 0, 0)),
                  pl.BlockSpec((r, V_HEAD_DIM), lambda h: (0, h))],
        out_specs=pl.BlockSpec((n_b, V_HEAD_DIM), lambda h: (row_block, h)),
        out_shape=jax.ShapeDtypeStruct(buf.shape, buf.dtype),
        input_output_aliases={0: 0},
        compiler_params=_params(1), name="expand_o")(buf, o_lat_t, w_uv2)


def _page_copies(pt_ref, ckv_hbm, kpt_hbm, ck_ring, kp_ring, sem_ck, sem_kp, chunk, n_groups):
    slot = chunk % PAGE_RING
    b = chunk // n_groups
    g = chunk % n_groups
    copies = []
    for i in range(PAGES_PER_STEP):
        pg = pt_ref[b, g * PAGES_PER_STEP + i]
        copies.append(pltpu.make_async_copy(ckv_hbm.at[pg], ck_ring.at[slot, i], sem_ck.at[slot]))
        copies.append(pltpu.make_async_copy(kpt_hbm.at[pg], kp_ring.at[slot, i], sem_kp.at[slot]))
    return copies


def _decode_kernel(pt_ref, ql_ref, qp_ref, cn_ref, kn_ref, ckv_hbm, kpt_hbm, o_ref,
                   ck_ring, kp_ring, sem_ck, sem_kp, ck_all, s_all):
    g = pl.program_id(1)
    n_groups = pl.num_programs(1)
    n_chunks = pl.num_programs(0) * n_groups
    chunk = pl.program_id(0) * n_groups + g
    page = ck_ring.shape[2]
    ring_args = (pt_ref, ckv_hbm, kpt_hbm, ck_ring, kp_ring, sem_ck, sem_kp)

    @pl.when(chunk == 0)
    def _():
        for c in range(PAGE_RING - 1):
            for cp in _page_copies(*ring_args, jnp.int32(c), n_groups):
                cp.start()

    @pl.when(chunk + (PAGE_RING - 1) < n_chunks)
    def _():
        for cp in _page_copies(*ring_args, chunk + (PAGE_RING - 1), n_groups):
            cp.start()

    for cp in _page_copies(*ring_args, chunk, n_groups):
        cp.wait()

    slot = chunk % PAGE_RING
    q = ql_ref[0]
    qp = qp_ref[0]
    for i in range(PAGES_PER_STEP):
        ck_page = ck_ring[slot, i].astype(BF16)
        ck_all[g, i * page:(i + 1) * page, :] = ck_page
        s_all[g, :, i * page:(i + 1) * page] = (
            lax.dot_general(q, ck_page, _NT, preferred_element_type=F32)
            + jnp.dot(qp, kp_ring[slot, i].astype(BF16), preferred_element_type=F32))

    @pl.when(g == n_groups - 1)
    def _():
        n_g = s_all.shape[0]
        c_new = cn_ref[0].astype(BF16).astype(F32)
        k_new = kn_ref[0].astype(BF16).astype(F32)
        s_self = (jnp.sum(q.astype(F32) * c_new, axis=-1, keepdims=True)
                  + jnp.sum(qp.astype(F32) * k_new, axis=-1, keepdims=True))
        m_fin = s_self
        for j in range(n_g):
            m_fin = jnp.maximum(m_fin, jnp.max(s_all[j], axis=-1, keepdims=True))
        p_self = jnp.exp2(s_self - m_fin)
        l_fin = p_self
        acc = p_self.astype(BF16).astype(F32) * c_new
        for j in range(n_g):
            p = jnp.exp2(s_all[j] - m_fin)
            l_fin = l_fin + jnp.sum(p, axis=-1, keepdims=True)
            acc = acc + jnp.dot(p.astype(BF16), ck_all[j], preferred_element_type=F32)
        o_ref[0] = (acc / l_fin).astype(o_ref.dtype)


def decode_attention(page_table, q_lat, q_pe, c_new, kpe_new, cache_ckv, cache_kpe_t):
    n_b, n_h, r = q_lat.shape
    p_dim = q_pe.shape[2]
    page = cache_ckv.shape[1]
    n_pages = page_table.shape[1]
    g_pages = PAGES_PER_STEP
    n_groups = n_pages // g_pages
    assert n_groups * g_pages == n_pages and n_b * n_groups >= PAGE_RING - 1

    in_specs = [pl.BlockSpec((1, n_h, r), lambda b, g, pt: (b, 0, 0)),
                pl.BlockSpec((1, n_h, p_dim), lambda b, g, pt: (b, 0, 0)),
                pl.BlockSpec((1, 1, r), lambda b, g, pt: (b, 0, 0)),
                pl.BlockSpec((1, 1, p_dim), lambda b, g, pt: (b, 0, 0)),
                pl.BlockSpec(memory_space=pl.ANY),
                pl.BlockSpec(memory_space=pl.ANY)]
    grid_spec = pltpu.PrefetchScalarGridSpec(
        num_scalar_prefetch=1, grid=(n_b, n_groups), in_specs=in_specs,
        out_specs=pl.BlockSpec((1, n_h, r), lambda b, g, pt: (b, 0, 0)),
        scratch_shapes=[pltpu.VMEM((PAGE_RING, g_pages, page, r), F32),
                        pltpu.VMEM((PAGE_RING, g_pages, p_dim, page), F32),
                        pltpu.SemaphoreType.DMA((PAGE_RING,)),
                        pltpu.SemaphoreType.DMA((PAGE_RING,)),
                        pltpu.VMEM((n_groups, g_pages * page, r), BF16),
                        pltpu.VMEM((n_groups, n_h, g_pages * page), F32)])
    return pl.pallas_call(
        _decode_kernel, grid_spec=grid_spec,
        out_shape=jax.ShapeDtypeStruct((n_b, n_h, r), BF16),
        compiler_params=_params(2), name="decode_attention")(
            page_table, q_lat, q_pe, c_new, kpe_new, cache_ckv, cache_kpe_t)


def _comb_identity(accs, ex):
    return (accs[0],)


def _comb_glu(accs, ex):
    return ((accs[0] + ex[0]) * _sigmoid(accs[1] + ex[1]),)


def _comb_short_conv_in(accs, ex):
    return (accs[0], accs[1] * accs[2])


def _comb_rms(accs, ex):
    y = _rms(accs[0], ex[0])
    return (y, y)


def _tile_lanes(tab, width):
    reps = width // tab.shape[1]
    return tab if reps == 1 else jnp.concatenate([tab] * reps, axis=-1)


def _comb_rope(scale, accs, ex):
    cos_t = _tile_lanes(ex[0], accs[0].shape[1])
    sin_t = _tile_lanes(ex[1], accs[0].shape[1])
    return ((accs[0] * cos_t + accs[1] * sin_t) * scale,)


def _comb_rope_f32_bf16(accs, ex):
    y = accs[0] * ex[0] + accs[1] * ex[1]
    return (y, y)


def _comb_scale(scale, accs, ex):
    return (accs[0] * scale,)


def _rot_cols(w):
    half = w.shape[-1] // 2
    return jnp.concatenate([-w[..., half:], w[..., :half]], axis=-1)


def _last_rows(u, n_batch, seq, n_keep):
    return jnp.stack([u[(b + 1) * seq - n_keep:(b + 1) * seq] for b in range(n_batch)])


def _pad_lanes(w):
    pad = LANES - w.shape[-1]
    return jnp.concatenate([w, jnp.zeros(w.shape[:-1] + (pad,), w.dtype)], axis=-1)


def kernel(x_prompt, x_sample, state_conv_l0, cache_ckv_l1, cache_kpe_l1, state_sconv_l2, state_conv_l3, page_table, norm_g, ca_w_in, ca_b_in, ca_dw, ca_dw_b, ca_ln_g, ca_ln_b, ca_w_out, ca_b_out, mla_w_dq, mla_q_norm, mla_w_uq, mla_w_dkv, mla_kv_norm, mla_w_uk, mla_w_uv, mla_w_o, sc_w_in, sc_w_conv, sc_w_out, ffn_w_gu, ffn_w_down):
    b_p, s_p, d = x_prompt.shape
    b_s = x_sample.shape[0]
    m_p = b_p * s_p
    depth = norm_g.shape[0]
    past_len = page_table.shape[1] * cache_ckv_l1.shape[1]
    conv_state_in = (state_conv_l0, state_conv_l3)
    tm = ROW_TILE
    sample_block = m_p // b_s

    gains = norm_g.reshape(depth * 4, 1, d)
    ca_b_in3 = ca_b_in[:, None, :]
    ca_b_out3 = ca_b_out[:, None, :]
    ca_dw_b3 = ca_dw_b[:, None, :]
    ca_ln_g3 = ca_ln_g[:, None, :]
    ca_ln_b3 = ca_ln_b[:, None, :]

    x, h = stack_and_norm(x_prompt.reshape(m_p, d), x_sample.reshape(b_s, d), gains, 0, tm=tm)

    half = QK_ROPE_DIM // 2
    inv = ROPE_THETA ** (-jnp.arange(half, dtype=F32) / half)
    pos = jnp.concatenate([jnp.arange(s_p), jnp.full((1,), past_len, jnp.int32)])
    ang = pos.astype(F32)[:, None] * inv[None, :]

    def rope_table(t):
        t = _pad_lanes(jnp.concatenate([t, t], axis=-1))
        return jnp.concatenate([jnp.tile(t[:s_p], (b_p, 1)), jnp.tile(t[s_p:], (b_s, 1))], axis=0)

    cos_tab = rope_table(jnp.cos(ang))
    sin_tab = rope_table(jnp.sin(ang))

    outs_conv = []
    for i in range(depth):
        kind = i % N_MIXERS
        g_post, g_ffn = 4 * i + 1, 4 * i + 2
        if kind == 0:
            j = i // N_MIXERS
            nb = d // 512
            (u,) = multi_mm("conf_in", h, [(ca_w_in, j, 0), (ca_w_in, j, nb)],
                            [(ca_b_in3, "col", j, 0), (ca_b_in3, "col", j, nb)],
                            _comb_glu, [F32], n_cols=d, tm=tm, tn=512)
            v = conf_conv_prompt(u, ca_dw, ca_dw_b3, ca_ln_g3, ca_ln_b3, j, batch=b_p, seq=s_p, ts=256)
            state_t = jnp.transpose(conv_state_in[j], (1, 0, 2))
            v = conf_conv_sample(v, state_t, u, ca_dw, ca_dw_b3, ca_ln_g3, ca_ln_b3, j,
                                 first_row=m_p, tb=16)
            x, h = out_proj("conf_out", v, ca_w_out, j, ca_b_out3, x, gains, g_post, g_ffn,
                            tm=OUT_PROJ_TILE)
            n_keep = ca_dw.shape[1] - 1
            u_s = u[m_p:]
            conv_p = _last_rows(u, b_p, s_p, n_keep)
            conv_s = jnp.concatenate([conv_state_in[j][:, 1:], u_s[:, None, :]], axis=1)
            outs_conv.append((conv_p, conv_s))
        elif kind == 1:
            r = KV_LORA_RANK
            qr = mla_w_dq.shape[1]
            w_down = jnp.concatenate([mla_w_dq, mla_w_dkv[:, :r]], axis=1)[None]
            g_down = jnp.concatenate([mla_q_norm, mla_kv_norm])[None, None]
            lat_f32, lat_bf = multi_mm("mla_down", h, [(w_down, 0, 0)], [(g_down, "col", 0, 0)],
                                       _comb_rms, [F32, BF16], n_cols=qr + r, tm=tm, tn=512)
            w_kpe = mla_w_dkv[:, r:]
            kpe, kpe_bf = multi_mm("mla_kpe", h,
                                   [(_pad_lanes(w_kpe)[None], 0, 0), (_pad_lanes(_rot_cols(w_kpe))[None], 0, 0)],
                                   [(cos_tab, "row", None, None), (sin_tab, "row", None, None)],
                                   _comb_rope_f32_bf16, [F32, BF16], n_cols=LANES, tm=tm, tn=LANES)
            cq = lat_bf[:, :qr]
            ckv_bf = lat_bf[:, qr:]
            w_uq3 = mla_w_uq.reshape(qr, N_HEADS, QK_NOPE_DIM + QK_ROPE_DIM)
            w_qn = w_uq3[:, :, :QK_NOPE_DIM].reshape(1, qr, N_HEADS * QK_NOPE_DIM)
            w_qp = w_uq3[:, :, QK_NOPE_DIM:]
            w_qp_a = _pad_lanes(w_qp).reshape(1, qr, N_HEADS * LANES)
            w_qp_b = _pad_lanes(_rot_cols(w_qp)).reshape(1, qr, N_HEADS * LANES)
            (q_nope,) = multi_mm("mla_qn", cq, [(w_qn, 0, 0)], [],
                                 functools.partial(_comb_scale, Q_SCALE), [BF16],
                                 n_cols=N_HEADS * QK_NOPE_DIM, tm=tm, tn=1024)
            (q_pe,) = multi_mm("mla_qp", cq, [(w_qp_a, 0, 0), (w_qp_b, 0, 0)],
                               [(cos_tab, "row", None, None), (sin_tab, "row", None, None)],
                               functools.partial(_comb_rope, Q_SCALE), [BF16],
                               n_cols=N_HEADS * LANES, tm=tm, tn=1024)
            w_uk2 = mla_w_uk.reshape(r, N_HEADS * QK_NOPE_DIM)
            w_uv2 = mla_w_uv.reshape(r, N_HEADS * V_HEAD_DIM)
            w_up = jnp.concatenate([w_uk2, w_uv2], axis=1)[None]
            (kv_up,) = multi_mm("mla_kv_up", ckv_bf, [(w_up, 0, 0)], [], _comb_identity, [BF16],
                                n_cols=w_up.shape[2], tm=tm, tn=1024)
            att = flash_prompt(q_nope, q_pe, kv_up, kpe_bf, batch=b_p, seq=s_p)
            q_lat = absorb_q(q_nope, w_uk2, row_block=sample_block, n_rows=b_s).transpose(1, 0, 2)
            q_pe_s = q_pe[m_p:].reshape(b_s, N_HEADS, LANES)[:, :, :QK_ROPE_DIM]
            ckv_s = lat_f32[m_p:, qr:].reshape(b_s, 1, r)
            kpe_s = kpe[m_p:, :QK_ROPE_DIM].reshape(b_s, 1, QK_ROPE_DIM)
            cache_kpe_t = jnp.swapaxes(cache_kpe_l1, 1, 2)
            o_lat = decode_attention(page_table, q_lat, q_pe_s, ckv_s, kpe_s, cache_ckv_l1, cache_kpe_t)
            att = expand_o(att, o_lat.transpose(1, 0, 2), w_uv2, row_block=sample_block)
            x, h = out_proj("mla_out", att, mla_w_o[None], 0, None, x, gains, g_post, g_ffn,
                            tm=OUT_PROJ_TILE)
            ckv_p = lat_f32[:m_p, qr:].reshape(b_p, s_p, r)
            kpe_p = kpe[:m_p, :QK_ROPE_DIM].reshape(b_p, s_p, QK_ROPE_DIM)
        else:
            nb = d // 512
            sc_w_in3 = sc_w_in[None]
            gate, u = multi_mm("sc_in", h, [(sc_w_in3, 0, 0), (sc_w_in3, 0, nb), (sc_w_in3, 0, 2 * nb)], [],
                               _comb_short_conv_in, [F32, F32], n_cols=d, tm=tm, tn=512)
            v = short_conv_prompt(u, gate, sc_w_conv, batch=b_p, seq=s_p, ts=256)
            v = short_conv_sample(v, state_sconv_l2, u, gate, sc_w_conv, first_row=m_p, tb=32)
            x, h = out_proj("sc_out", v, sc_w_out[None], 0, None, x, gains, g_post, g_ffn,
                            tm=OUT_PROJ_TILE)
            n_keep = sc_w_conv.shape[0] - 1
            u_s = u[m_p:]
            sc_p = _last_rows(u, b_p, s_p, n_keep)
            sc_s = jnp.concatenate([state_sconv_l2[:, 1:], u_s[:, None, :]], axis=1)

        g_next = 4 * (i + 1) if i + 1 < depth else None
        x, h = ffn(h, x, ffn_w_gu, ffn_w_down, i, gains, 4 * i + 3, g_next,
                   tm=tm, tf=FFN_CHUNK, n_tail=b_s)

    y_prompt = x.reshape(b_p, s_p, d)
    y_sample = h.reshape(b_s, 1, d)
    return (y_prompt, y_sample, outs_conv[0][0], outs_conv[0][1], ckv_p, kpe_p,
            ckv_s, kpe_s, sc_p, sc_s, outs_conv[1][0], outs_conv[1][1])
```

```python
import functools

import jax
import jax.numpy as jnp
from jax import lax
from jax.experimental import pallas as pl
from jax.experimental.pallas import tpu as pltpu

F32 = jnp.float32
BF16 = jnp.bfloat16

N_HEADS = 16
QK_NOPE_DIM = 128
QK_ROPE_DIM = 64
V_HEAD_DIM = 128
KV_LORA_RANK = 512
ROPE_THETA = 10000.0
ATTN_SCALE = (QK_NOPE_DIM + QK_ROPE_DIM) ** -0.5
LOG2_E = 1.4426950408889634
Q_SCALE = ATTN_SCALE * LOG2_E
NORM_EPS = 1e-6
N_MIXERS = 3
LANES = 128

VMEM_LIMIT_BYTES = 58 * 1024 * 1024

ROW_TILE = 1040
OUT_PROJ_TILE = 416
OUT_PROJ_SPLIT = 2
FFN_CHUNK = 256

_SINGLE = pl.Buffered(1)
_NT = (((1,), (1,)), ((), ()))
_TN = (((0,), (0,)), ((), ()))


def _params(n_grid_axes):
    return pltpu.CompilerParams(
        dimension_semantics=("arbitrary",) * n_grid_axes,
        vmem_limit_bytes=VMEM_LIMIT_BYTES)


def _sigmoid(x):
    return 1.0 / (1.0 + jnp.exp(-x))


def _rms(x, g):
    return x * lax.rsqrt(jnp.mean(x * x, axis=-1, keepdims=True) + NORM_EPS) * g


def _layer_norm_silu(v, g, beta):
    mu = jnp.mean(v, axis=-1, keepdims=True)
    var = jnp.mean(jnp.square(v - mu), axis=-1, keepdims=True)
    y = (v - mu) * lax.rsqrt(var + NORM_EPS) * g + beta
    return y * _sigmoid(y)


def _stack_kernel(n_tail, xp_ref, xs_ref, g_ref, x_ref, h_ref):
    x_ref[...] = xp_ref[...]

    @pl.when(pl.program_id(0) == pl.num_programs(0) - 1)
    def _():
        x_ref[x_ref.shape[0] - n_tail:, :] = xs_ref[...]

    x = x_ref[...]
    h_ref[...] = _rms(x, g_ref[...]).astype(h_ref.dtype)


def stack_and_norm(xp, xs, gains, g_idx, *, tm):
    m_p, d = xp.shape
    m_s = xs.shape[0]
    m = m_p + m_s
    assert m % tm == 0 and m - m_p == m_s and tm - m_s >= 0 and (m_p % tm) == tm - m_s
    return pl.pallas_call(
        functools.partial(_stack_kernel, m_s),
        grid=(m // tm,),
        in_specs=[pl.BlockSpec((tm, d), lambda i: (i, 0)),
                  pl.BlockSpec((m_s, d), lambda i: (0, 0)),
                  pl.BlockSpec((None, 1, d), lambda i: (g_idx, 0, 0))],
        out_specs=[pl.BlockSpec((tm, d), lambda i: (i, 0)), pl.BlockSpec((tm, d), lambda i: (i, 0))],
        out_shape=[jax.ShapeDtypeStruct((m, d), F32), jax.ShapeDtypeStruct((m, d), BF16)],
        compiler_params=_params(1), name="stack_and_norm")(xp, xs, gains)


def _mm_kernel(n_w, n_ex, combine, x_ref, *refs):
    w_refs = refs[:n_w]
    ex_refs = refs[n_w:n_w + n_ex]
    out_refs = refs[n_w + n_ex:len(refs) - n_w]
    wb_refs = refs[len(refs) - n_w:]

    @pl.when(pl.program_id(1) == 0)
    def _():
        for w, wb in zip(w_refs, wb_refs):
            wb[...] = w[...].astype(BF16)

    x = x_ref[...]
    accs = [jnp.dot(x, wb[...], preferred_element_type=F32) for wb in wb_refs]
    outs = combine(accs, [e[...] for e in ex_refs])
    for o_ref, o in zip(out_refs, outs):
        o_ref[...] = o.astype(o_ref.dtype)


def multi_mm(name, x, ws, extras, combine, out_dtypes, *, n_cols, tm, tn, x_cols=None):
    m = x.shape[0]
    x_blk, k = (0, x.shape[1]) if x_cols is None else x_cols
    in_specs = [pl.BlockSpec((tm, k), lambda j, i: (i, x_blk))]
    args = [x]
    for w, layer, off in ws:
        in_specs.append(pl.BlockSpec((None, k, tn), lambda j, i, layer=layer, off=off: (layer, 0, j + off)))
        args.append(w)
    for arr, kind, layer, off in extras:
        if kind == "row":
            in_specs.append(pl.BlockSpec((tm, arr.shape[1]), lambda j, i: (i, 0)))
        else:
            in_specs.append(pl.BlockSpec((None, 1, tn), lambda j, i, layer=layer, off=off: (layer, 0, j + off)))
        args.append(arr)
    out_shape = [jax.ShapeDtypeStruct((m, n_cols), dt) for dt in out_dtypes]
    out_specs = [pl.BlockSpec((tm, tn), lambda j, i: (i, j)) for _ in out_dtypes]
    return pl.pallas_call(
        functools.partial(_mm_kernel, len(ws), len(extras), combine),
        grid=(n_cols // tn, m // tm), in_specs=in_specs, out_specs=out_specs, out_shape=out_shape,
        scratch_shapes=[pltpu.VMEM((k, tn), BF16) for _ in ws],
        compiler_params=_params(2), name=name)(*args)


def _out_proj_kernel(has_bias, v_ref, w_ref, *refs):
    if has_bias:
        b_ref, refs = refs[0], refs[1:]
    x_ref, g1_ref, g2_ref, xo_ref, h_ref, wb_ref = refs

    @pl.when(pl.program_id(0) == 0)
    def _():
        wb_ref[...] = w_ref[...].astype(BF16)

    rows = v_ref.shape[0] // OUT_PROJ_SPLIT
    for s in range(OUT_PROJ_SPLIT):
        sl = slice(s * rows, (s + 1) * rows)
        y = jnp.dot(v_ref[sl, :], wb_ref[...], preferred_element_type=F32)
        if has_bias:
            y = y + b_ref[...]
        x_new = x_ref[sl, :] + _rms(y, g1_ref[...])
        xo_ref[sl, :] = x_new
        h_ref[sl, :] = _rms(x_new, g2_ref[...]).astype(h_ref.dtype)


def out_proj(name, v, w, layer, bias, x, gains, g_post, g_next, *, tm):
    m, k = v.shape
    d = w.shape[2]
    in_specs = [pl.BlockSpec((tm, k), lambda i: (i, 0)),
                pl.BlockSpec((None, k, d), lambda i: (layer, 0, 0), pipeline_mode=_SINGLE)]
    args = [v, w]
    if bias is not None:
        in_specs.append(pl.BlockSpec((None, 1, d), lambda i: (layer, 0, 0)))
        args.append(bias)
    in_specs += [pl.BlockSpec((tm, d), lambda i: (i, 0)),
                 pl.BlockSpec((None, 1, d), lambda i: (g_post, 0, 0)),
                 pl.BlockSpec((None, 1, d), lambda i: (g_next, 0, 0))]
    args += [x, gains, gains]
    return pl.pallas_call(
        functools.partial(_out_proj_kernel, bias is not None),
        grid=(m // tm,), in_specs=in_specs,
        out_specs=[pl.BlockSpec((tm, d), lambda i: (i, 0)), pl.BlockSpec((tm, d), lambda i: (i, 0))],
        out_shape=[jax.ShapeDtypeStruct((m, d), F32), jax.ShapeDtypeStruct((m, d), BF16)],
        scratch_shapes=[pltpu.VMEM((k, d), BF16)],
        compiler_params=_params(1), name=name)(*args)


def _ffn_partial(h_ref, wg_ref, wu_ref, wd_ref):
    h = h_ref[...]
    g = jnp.dot(h, wg_ref[...].astype(BF16), preferred_element_type=F32)
    u = jnp.dot(h, wu_ref[...].astype(BF16), preferred_element_type=F32)
    act = (g * _sigmoid(g) * u).astype(BF16)
    return jnp.dot(act, wd_ref[...].astype(BF16), preferred_element_type=F32)


def _ffn_kernel(h_ref, x_ref, wg_ref, wu_ref, wd_ref, g1_ref, g2_ref, xo_ref, hn_ref):
    f = pl.program_id(1)

    @pl.when(f == 0)
    def _():
        xo_ref[...] = jnp.zeros(xo_ref.shape, F32)

    xo_ref[...] += _ffn_partial(h_ref, wg_ref, wu_ref, wd_ref)

    @pl.when(f == pl.num_programs(1) - 1)
    def _():
        x_new = x_ref[...] + _rms(xo_ref[...], g1_ref[...])
        xo_ref[...] = x_new
        hn_ref[...] = _rms(x_new, g2_ref[...]).astype(hn_ref.dtype)


def _ffn_last_kernel(n_tail, h_ref, x_ref, wg_ref, wu_ref, wd_ref, g1_ref, yp_ref, ys_ref):
    f = pl.program_id(1)

    @pl.when(f == 0)
    def _():
        yp_ref[...] = jnp.zeros(yp_ref.shape, F32)

    yp_ref[...] += _ffn_partial(h_ref, wg_ref, wu_ref, wd_ref)

    @pl.when(f == pl.num_programs(1) - 1)
    def _():
        yp_ref[...] = x_ref[...] + _rms(yp_ref[...], g1_ref[...])

        @pl.when(pl.program_id(0) == pl.num_programs(0) - 1)
        def _():
            ys_ref[...] = yp_ref[yp_ref.shape[0] - n_tail:, :]


def ffn(h, x, w_gu, w_down, layer, gains, g_post, g_next, *, tm, tf, n_tail):
    m, d = h.shape
    hidden = w_down.shape[1]
    n_f = hidden // tf
    in_specs = [pl.BlockSpec((tm, d), lambda i, f: (i, 0), pipeline_mode=_SINGLE),
                pl.BlockSpec((tm, d), lambda i, f: (i, 0)),
                pl.BlockSpec((None, d, tf), lambda i, f: (layer, 0, f)),
                pl.BlockSpec((None, d, tf), lambda i, f: (layer, 0, f + n_f)),
                pl.BlockSpec((None, tf, d), lambda i, f: (layer, f, 0)),
                pl.BlockSpec((None, 1, d), lambda i, f: (g_post, 0, 0))]
    args = [h, x, w_gu, w_gu, w_down, gains]
    row_spec = lambda: pl.BlockSpec((tm, d), lambda i, f: (i, 0), pipeline_mode=_SINGLE)
    if g_next is not None:
        in_specs.append(pl.BlockSpec((None, 1, d), lambda i, f: (g_next, 0, 0)))
        args.append(gains)
        return pl.pallas_call(
            _ffn_kernel, grid=(m // tm, n_f), in_specs=in_specs,
            out_specs=[pl.BlockSpec((tm, d), lambda i, f: (i, 0)), row_spec()],
            out_shape=[jax.ShapeDtypeStruct((m, d), F32), jax.ShapeDtypeStruct((m, d), BF16)],
            compiler_params=_params(2), name="ffn")(*args)
    assert (m - n_tail) % tm == tm - n_tail
    return pl.pallas_call(
        functools.partial(_ffn_last_kernel, n_tail), grid=(m // tm, n_f), in_specs=in_specs,
        out_specs=[row_spec(), pl.BlockSpec((n_tail, d), lambda i, f: (0, 0))],
        out_shape=[jax.ShapeDtypeStruct((m - n_tail, d), F32), jax.ShapeDtypeStruct((n_tail, d), F32)],
        compiler_params=_params(2), name="ffn_last")(*args)


CONV_HALO = 32
CONV_ROWS = 128
SUBLANES = 8


def _conf_conv_kernel(ts, n_taps, halo_ref, main_ref, w_ref, b_ref, g_ref, beta_ref, o_ref,
                      win_ref, conv_ref, w8_ref):
    t = pl.program_id(1)
    d = main_ref.shape[1]
    n_c = d // LANES
    n_sub = CONV_ROWS // SUBLANES
    col_slices = [slice(c * LANES, (c + 1) * LANES) for c in range(n_c)]

    @pl.when((pl.program_id(0) == 0) & (t == 0))
    def _():
        for k in range(n_taps):
            for c, cols in enumerate(col_slices):
                w8_ref[k, c] = jnp.broadcast_to(w_ref[k:k + 1, cols], (SUBLANES, LANES))

    for c, cols in enumerate(col_slices):
        win_ref[c, 0:CONV_HALO, :] = jnp.where(t == 0, 0.0, halo_ref[:, cols])
        win_ref[c, CONV_HALO:, :] = main_ref[:, cols]
    first = CONV_HALO - (n_taps - 1)

    def column(c, carry):
        def row_chunk(rc, carry2):
            r0 = pl.multiple_of(rc * CONV_ROWS, CONV_ROWS)
            accs = [None] * n_sub
            for k in range(n_taps):
                src = win_ref[c, pl.ds(r0 + first + k, CONV_ROWS), :]
                wk = w8_ref[k, c]
                for rr in range(n_sub):
                    term = wk * src[rr * SUBLANES:(rr + 1) * SUBLANES, :]
                    accs[rr] = term if k == 0 else accs[rr] + term
            conv_ref[c, pl.ds(r0, CONV_ROWS), :] = jnp.concatenate(accs, axis=0)
            return carry2

        return lax.fori_loop(0, ts // CONV_ROWS, row_chunk, carry)

    lax.fori_loop(0, n_c, column, 0)

    vs = [conv_ref[c] + b_ref[:, cols] for c, cols in enumerate(col_slices)]
    mu = jnp.sum(functools.reduce(jnp.add, vs), axis=-1, keepdims=True) / d
    sq = functools.reduce(jnp.add, [jnp.square(v - mu) for v in vs])
    rstd = lax.rsqrt(jnp.sum(sq, axis=-1, keepdims=True) / d + NORM_EPS)
    for v, cols in zip(vs, col_slices):
        y = (v - mu) * rstd * g_ref[:, cols] + beta_ref[:, cols]
        o_ref[:, cols] = (y * _sigmoid(y)).astype(o_ref.dtype)


def conf_conv_prompt(u, dw, dw_b, ln_g, ln_b, layer, *, batch, seq, ts):
    m, d = u.shape
    n_taps = dw.shape[1]
    n_t = seq // ts
    halo_per_tile = ts // CONV_HALO
    vec = lambda: pl.BlockSpec((None, 1, d), lambda b, t: (layer, 0, 0))
    return pl.pallas_call(
        functools.partial(_conf_conv_kernel, ts, n_taps),
        grid=(batch, n_t),
        in_specs=[
            pl.BlockSpec((CONV_HALO, d),
                         lambda b, t: (jnp.maximum((b * n_t + t) * halo_per_tile - 1, 0), 0)),
            pl.BlockSpec((ts, d), lambda b, t: (b * n_t + t, 0)),
            pl.BlockSpec((None, n_taps, d), lambda b, t: (layer, 0, 0)),
            vec(), vec(), vec()],
        out_specs=pl.BlockSpec((ts, d), lambda b, t: (b * n_t + t, 0)),
        out_shape=jax.ShapeDtypeStruct((m, d), BF16),
        scratch_shapes=[pltpu.VMEM((d // LANES, ts + CONV_HALO, LANES), F32),
                        pltpu.VMEM((d // LANES, ts, LANES), F32),
                        pltpu.VMEM((n_taps, d // LANES, SUBLANES, LANES), F32)],
        compiler_params=_params(2), name="conf_conv_prompt")(u, u, dw, dw_b, ln_g, ln_b)


SC_HALO = 8


def _short_conv_kernel(ts, n_taps, halo_ref, main_ref, gate_ref, w_ref, o_ref, win_ref):
    t = pl.program_id(1)
    d = main_ref.shape[1]
    first = SC_HALO - (n_taps - 1)
    for c in range(d // LANES):
        cols = slice(c * LANES, (c + 1) * LANES)
        win_ref[c, 0:SC_HALO, :] = jnp.where(t == 0, 0.0, halo_ref[:, cols])
        win_ref[c, SC_HALO:, :] = main_ref[:, cols]
    for c in range(d // LANES):
        cols = slice(c * LANES, (c + 1) * LANES)
        acc = w_ref[0:1, cols] * win_ref[c, first:first + ts, :]
        for k in range(1, n_taps):
            acc = acc + w_ref[k:k + 1, cols] * win_ref[c, first + k:first + k + ts, :]
        o_ref[:, cols] = (gate_ref[:, cols] * acc).astype(o_ref.dtype)


def short_conv_prompt(u, gate, w_conv, *, batch, seq, ts):
    m, d = u.shape
    n_taps = w_conv.shape[0]
    n_t = seq // ts
    halo_per_tile = ts // SC_HALO
    return pl.pallas_call(
        functools.partial(_short_conv_kernel, ts, n_taps),
        grid=(batch, n_t),
        in_specs=[
            pl.BlockSpec((SC_HALO, d),
                         lambda b, t: (jnp.maximum((b * n_t + t) * halo_per_tile - 1, 0), 0)),
            pl.BlockSpec((ts, d), lambda b, t: (b * n_t + t, 0)),
            pl.BlockSpec((ts, d), lambda b, t: (b * n_t + t, 0)),
            pl.BlockSpec((n_taps, d), lambda b, t: (0, 0))],
        out_specs=pl.BlockSpec((ts, d), lambda b, t: (b * n_t + t, 0)),
        out_shape=jax.ShapeDtypeStruct((m, d), BF16),
        scratch_shapes=[pltpu.VMEM((d // LANES, ts + SC_HALO, LANES), F32)],
        compiler_params=_params(2), name="short_conv_prompt")(u, u, gate, w_conv)


def _conf_conv_sample_kernel(n_hist, buf_ref, st_ref, u_ref, w_ref, b_ref, g_ref, beta_ref,
                             o_ref, ns_ref):
    del buf_ref
    u = u_ref[...]
    acc = w_ref[n_hist:n_hist + 1, :] * u + b_ref[...]
    for k in range(n_hist):
        acc = acc + w_ref[k:k + 1, :] * st_ref[k]
    o_ref[...] = _layer_norm_silu(acc, g_ref[...], beta_ref[...]).astype(o_ref.dtype)
    ns_ref[0:n_hist - 1] = st_ref[1:n_hist]
    ns_ref[n_hist - 1] = u


def conf_conv_sample(buf, state_t, u, dw, dw_b, ln_g, ln_b, layer, *, first_row, tb):
    n_hist, n_b, d = state_t.shape
    blk0 = first_row // tb
    vec = lambda: pl.BlockSpec((None, 1, d), lambda i: (layer, 0, 0))
    return pl.pallas_call(
        functools.partial(_conf_conv_sample_kernel, n_hist),
        grid=(n_b // tb,),
        in_specs=[pl.BlockSpec(memory_space=pl.ANY),
                  pl.BlockSpec((n_hist, tb, d), lambda i: (0, i, 0)),
                  pl.BlockSpec((tb, d), lambda i: (blk0 + i, 0)),
                  pl.BlockSpec((None, n_hist + 1, d), lambda i: (layer, 0, 0)),
                  vec(), vec(), vec()],
        out_specs=[pl.BlockSpec((tb, d), lambda i: (blk0 + i, 0)),
                   pl.BlockSpec((n_hist, tb, d), lambda i: (0, i, 0))],
        out_shape=[jax.ShapeDtypeStruct(buf.shape, buf.dtype),
                   jax.ShapeDtypeStruct(state_t.shape, state_t.dtype)],
        input_output_aliases={0: 0},
        compiler_params=_params(1), name="conf_conv_sample")(buf, state_t, u, dw, dw_b, ln_g, ln_b)


def _short_conv_sample_kernel(n_hist, buf_ref, st_ref, u_ref, gate_ref, w_ref, o_ref):
    del buf_ref
    d = u_ref.shape[1]
    acc = w_ref[n_hist:n_hist + 1, :] * u_ref[...]
    for k in range(n_hist):
        acc = acc + w_ref[k:k + 1, :] * st_ref[:, k * d:(k + 1) * d]
    o_ref[...] = (gate_ref[...] * acc).astype(o_ref.dtype)


def short_conv_sample(buf, state, u, gate, w_conv, *, first_row, tb):
    n_b, n_hist, d = state.shape
    st2 = state.reshape(n_b, n_hist * d)
    blk0 = first_row // tb
    return pl.pallas_call(
        functools.partial(_short_conv_sample_kernel, n_hist),
        grid=(n_b // tb,),
        in_specs=[pl.BlockSpec(memory_space=pl.ANY),
                  pl.BlockSpec((tb, n_hist * d), lambda i: (i, 0)),
                  pl.BlockSpec((tb, d), lambda i: (blk0 + i, 0)),
                  pl.BlockSpec((tb, d), lambda i: (blk0 + i, 0)),
                  pl.BlockSpec((n_hist + 1, d), lambda i: (0, 0))],
        out_specs=pl.BlockSpec((tb, d), lambda i: (blk0 + i, 0)),
        out_shape=jax.ShapeDtypeStruct(buf.shape, buf.dtype),
        input_output_aliases={0: 0},
        compiler_params=_params(1), name="short_conv_sample")(buf, st2, u, gate, w_conv)


ATT_TQ = 512
ATT_TK = 512


def _flash_kernel(seq, qn_ref, qp_ref, kn_ref, kp_ref, v_ref, o_ref):
    k_full = jnp.concatenate([kn_ref[...], kp_ref[...]], axis=-1)
    n_q = seq // ATT_TQ
    qs = [jnp.concatenate([qn_ref[i * ATT_TQ:(i + 1) * ATT_TQ, :],
                           qp_ref[i * ATT_TQ:(i + 1) * ATT_TQ, :]], axis=-1) for i in range(n_q)]
    m_run = [jnp.full((1, ATT_TQ), -jnp.inf, F32) for _ in range(n_q)]
    l_run = [jnp.zeros((1, ATT_TQ), F32) for _ in range(n_q)]
    acc = [jnp.zeros((o_ref.shape[1], ATT_TQ), F32) for _ in range(n_q)]
    pairs = [(ki, qi) for ki in range(seq // ATT_TK) for qi in range(n_q)
             if (qi + 1) * ATT_TQ > ki * ATT_TK]

    def scores_t(ki, qi):
        k0 = ki * ATT_TK
        st = lax.dot_general(k_full[k0:k0 + ATT_TK, :], qs[qi], _NT, preferred_element_type=F32)
        if k0 + ATT_TK > qi * ATT_TQ:
            key = k0 + lax.broadcasted_iota(jnp.int32, st.shape, 0)
            qpos = qi * ATT_TQ + lax.broadcasted_iota(jnp.int32, st.shape, 1)
            st = jnp.where(key <= qpos, st, -jnp.inf)
        return st

    st_next = scores_t(*pairs[0])
    for t, (ki, qi) in enumerate(pairs):
        st = st_next
        if t + 1 < len(pairs):
            st_next = scores_t(*pairs[t + 1])
        m_new = jnp.maximum(m_run[qi], jnp.max(st, axis=0, keepdims=True))
        corr = jnp.exp2(m_run[qi] - m_new)
        p = jnp.exp2(st - m_new)
        l_run[qi] = l_run[qi] * corr + jnp.sum(p, axis=0, keepdims=True)
        m_run[qi] = m_new
        acc[qi] = acc[qi] * corr + lax.dot_general(
            v_ref[ki * ATT_TK:(ki + 1) * ATT_TK, :], p.astype(BF16), _TN, preferred_element_type=F32)
    for qi in range(n_q):
        o_ref[qi * ATT_TQ:(qi + 1) * ATT_TQ, :] = (acc[qi] / l_run[qi]).T.astype(o_ref.dtype)


def flash_prompt(q_nope, q_pe, kv_up, kpe, *, batch, seq):
    hd = V_HEAD_DIM
    spec = lambda col: pl.BlockSpec((seq, hd), col)
    return pl.pallas_call(
        functools.partial(_flash_kernel, seq),
        grid=(batch, N_HEADS),
        in_specs=[spec(lambda b, h: (b, h)), spec(lambda b, h: (b, h)),
                  spec(lambda b, h: (b, h)), spec(lambda b, h: (b, 0)),
                  spec(lambda b, h: (b, h + N_HEADS))],
        out_specs=spec(lambda b, h: (b, h)),
        out_shape=jax.ShapeDtypeStruct((q_nope.shape[0], N_HEADS * hd), BF16),
        compiler_params=_params(2), name="flash_prompt")(q_nope, q_pe, kv_up, kpe, kv_up)


PAGES_PER_STEP = 16
PAGE_RING = 3


def _per_head_nt_kernel(x_ref, w_ref, o_ref):
    o_ref[0] = lax.dot_general(x_ref[...], w_ref[...].astype(BF16), _NT,
                               preferred_element_type=F32).astype(o_ref.dtype)


def absorb_q(q_nope, w_uk2, *, row_block, n_rows):
    r = w_uk2.shape[0]
    return pl.pallas_call(
        _per_head_nt_kernel, grid=(N_HEADS,),
        in_specs=[pl.BlockSpec((n_rows, QK_NOPE_DIM), lambda h: (row_block, h)),
                  pl.BlockSpec((r, QK_NOPE_DIM), lambda h: (0, h))],
        out_specs=pl.BlockSpec((1, n_rows, r), lambda h: (h, 0, 0)),
        out_shape=jax.ShapeDtypeStruct((N_HEADS, n_rows, r), BF16),
        compiler_params=_params(1), name="absorb_q")(q_nope, w_uk2)


def _per_head_nn_kernel(buf_ref, x_ref, w_ref, o_ref):
    del buf_ref
    o_ref[...] = jnp.dot(x_ref[0], w_ref[...].astype(BF16),
                         preferred_element_type=F32).astype(o_ref.dtype)


def expand_o(buf, o_lat_t, w_uv2, *, row_block):
    n_h, n_b, r = o_lat_t.shape
    return pl.pallas_call(
        _per_head_nn_kernel, grid=(n_h,),
        in_specs=[pl.BlockSpec(memory_space=pl.ANY),
                  pl.BlockSpec((1, n_b, r), lambda h: (h, 0, 0)),
                  pl.BlockSpec((r, V_HEAD_DIM), lambda h: (0, h))],
        out_specs=pl.BlockSpec((n_b, V_HEAD_DIM), lambda h: (row_block, h)),
        out_shape=jax.ShapeDtypeStruct(buf.shape, buf.dtype),
        input_output_aliases={0: 0},
        compiler_params=_params(1), name="expand_o")(buf, o_lat_t, w_uv2)


def _page_copies(pt_ref, ckv_hbm, kpt_hbm, ck_ring, kp_ring, sem_ck, sem_kp, chunk, n_groups):
    slot = chunk % PAGE_RING
    b = chunk // n_groups
    g = chunk % n_groups
    copies = []
    for i in range(PAGES_PER_STEP):
        pg = pt_ref[b, g * PAGES_PER_STEP + i]
        copies.append(pltpu.make_async_copy(ckv_hbm.at[pg], ck_ring.at[slot, i], sem_ck.at[slot]))
        copies.append(pltpu.make_async_copy(kpt_hbm.at[pg], kp_ring.at[slot, i], sem_kp.at[slot]))
    return copies


def _decode_kernel(pt_ref, ql_ref, qp_ref, cn_ref, kn_ref, ckv_hbm, kpt_hbm, o_ref,
                   ck_ring, kp_ring, sem_ck, sem_kp, ck_all, s_all):
    g = pl.program_id(1)
    n_groups = pl.num_programs(1)
    n_chunks = pl.num_programs(0) * n_groups
    chunk = pl.program_id(0) * n_groups + g
    page = ck_ring.shape[2]
    ring_args = (pt_ref, ckv_hbm, kpt_hbm, ck_ring, kp_ring, sem_ck, sem_kp)

    @pl.when(chunk == 0)
    def _():
        for c in range(PAGE_RING - 1):
            for cp in _page_copies(*ring_args, jnp.int32(c), n_groups):
                cp.start()

    @pl.when(chunk + (PAGE_RING - 1) < n_chunks)
    def _():
        for cp in _page_copies(*ring_args, chunk + (PAGE_RING - 1), n_groups):
            cp.start()

    for cp in _page_copies(*ring_args, chunk, n_groups):
        cp.wait()

    slot = chunk % PAGE_RING
    q = ql_ref[0]
    qp = qp_ref[0]
    for i in range(PAGES_PER_STEP):
        ck_page = ck_ring[slot, i].astype(BF16)
        ck_all[g, i * page:(i + 1) * page, :] = ck_page
        s_all[g, :, i * page:(i + 1) * page] = (
            lax.dot_general(q, ck_page, _NT, preferred_element_type=F32)
            + jnp.dot(qp, kp_ring[slot, i].astype(BF16), preferred_element_type=F32))

    @pl.when(g == n_groups - 1)
    def _():
        n_g = s_all.shape[0]
        c_new = cn_ref[0].astype(BF16).astype(F32)
        k_new = kn_ref[0].astype(BF16).astype(F32)
        s_self = (jnp.sum(q.astype(F32) * c_new, axis=-1, keepdims=True)
                  + jnp.sum(qp.astype(F32) * k_new, axis=-1, keepdims=True))
        m_fin = s_self
        for j in range(n_g):
            m_fin = jnp.maximum(m_fin, jnp.max(s_all[j], axis=-1, keepdims=True))
        p_self = jnp.exp2(s_self - m_fin)
        l_fin = p_self
        acc = p_self.astype(BF16).astype(F32) * c_new
        for j in range(n_g):
            p = jnp.exp2(s_all[j] - m_fin)
            l_fin = l_fin + jnp.sum(p, axis=-1, keepdims=True)
            acc = acc + jnp.dot(p.astype(BF16), ck_all[j], preferred_element_type=F32)
        o_ref[0] = (acc / l_fin).astype(o_ref.dtype)


def decode_attention(page_table, q_lat, q_pe, c_new, kpe_new, cache_ckv, cache_kpe_t):
    n_b, n_h, r = q_lat.shape
    p_dim = q_pe.shape[2]
    page = cache_ckv.shape[1]
    n_pages = page_table.shape[1]
    g_pages = PAGES_PER_STEP
    n_groups = n_pages // g_pages
    assert n_groups * g_pages == n_pages and n_b * n_groups >= PAGE_RING - 1

    in_specs = [pl.BlockSpec((1, n_h, r), lambda b, g, pt: (b, 0, 0)),
                pl.BlockSpec((1, n_h, p_dim), lambda b, g, pt: (b, 0, 0)),
                pl.BlockSpec((1, 1, r), lambda b, g, pt: (b, 0, 0)),
                pl.BlockSpec((1, 1, p_dim), lambda b, g, pt: (b, 0, 0)),
                pl.BlockSpec(memory_space=pl.ANY),
                pl.BlockSpec(memory_space=pl.ANY)]
    grid_spec = pltpu.PrefetchScalarGridSpec(
        num_scalar_prefetch=1, grid=(n_b, n_groups), in_specs=in_specs,
        out_specs=pl.BlockSpec((1, n_h, r), lambda b, g, pt: (b, 0, 0)),
        scratch_shapes=[pltpu.VMEM((PAGE_RING, g_pages, page, r), F32),
                        pltpu.VMEM((PAGE_RING, g_pages, p_dim, page), F32),
                        pltpu.SemaphoreType.DMA((PAGE_RING,)),
                        pltpu.SemaphoreType.DMA((PAGE_RING,)),
                        pltpu.VMEM((n_groups, g_pages * page, r), BF16),
                        pltpu.VMEM((n_groups, n_h, g_pages * page), F32)])
    return pl.pallas_call(
        _decode_kernel, grid_spec=grid_spec,
        out_shape=jax.ShapeDtypeStruct((n_b, n_h, r), BF16),
        compiler_params=_params(2), name="decode_attention")(
            page_table, q_lat, q_pe, c_new, kpe_new, cache_ckv, cache_kpe_t)


def _comb_identity(accs, ex):
    return (accs[0],)


def _comb_glu(accs, ex):
    return ((accs[0] + ex[0]) * _sigmoid(accs[1] + ex[1]),)


def _comb_short_conv_in(accs, ex):
    return (accs[0], accs[1] * accs[2])


def _comb_rms(accs, ex):
    y = _rms(accs[0], ex[0])
    return (y, y)


def _tile_lanes(tab, width):
    reps = width // tab.shape[1]
    return tab if reps == 1 else jnp.concatenate([tab] * reps, axis=-1)


def _comb_rope(scale, accs, ex):
    cos_t = _tile_lanes(ex[0], accs[0].shape[1])
    sin_t = _tile_lanes(ex[1], accs[0].shape[1])
    return ((accs[0] * cos_t + accs[1] * sin_t) * scale,)


def _comb_rope_f32_bf16(accs, ex):
    y = accs[0] * ex[0] + accs[1] * ex[1]
    return (y, y)


def _comb_scale(scale, accs, ex):
    return (accs[0] * scale,)


def _rot_cols(w):
    half = w.shape[-1] // 2
    return jnp.concatenate([-w[..., half:], w[..., :half]], axis=-1)


def _last_rows(u, n_batch, seq, n_keep):
    return jnp.stack([u[(b + 1) * seq - n_keep:(b + 1) * seq] for b in range(n_batch)])


def _pad_lanes(w):
    pad = LANES - w.shape[-1]
    return jnp.concatenate([w, jnp.zeros(w.shape[:-1] + (pad,), w.dtype)], axis=-1)


def kernel(x_prompt, x_sample, state_conv_l0, cache_ckv_l1, cache_kpe_l1, state_sconv_l2, state_conv_l3, page_table, norm_g, ca_w_in, ca_b_in, ca_dw, ca_dw_b, ca_ln_g, ca_ln_b, ca_w_out, ca_b_out, mla_w_dq, mla_q_norm, mla_w_uq, mla_w_dkv, mla_kv_norm, mla_w_uk, mla_w_uv, mla_w_o, sc_w_in, sc_w_conv, sc_w_out, ffn_w_gu, ffn_w_down):
    b_p, s_p, d = x_prompt.shape
    b_s = x_sample.shape[0]
    m_p = b_p * s_p
    depth = norm_g.shape[0]
    past_len = page_table.shape[1] * cache_ckv_l1.shape[1]
    conv_state_in = (state_conv_l0, state_conv_l3)
    tm = ROW_TILE
    sample_block = m_p // b_s

    gains = norm_g.reshape(depth * 4, 1, d)
    ca_b_in3 = ca_b_in[:, None, :]
    ca_b_out3 = ca_b_out[:, None, :]
    ca_dw_b3 = ca_dw_b[:, None, :]
    ca_ln_g3 = ca_ln_g[:, None, :]
    ca_ln_b3 = ca_ln_b[:, None, :]

    x, h = stack_and_norm(x_prompt.reshape(m_p, d), x_sample.reshape(b_s, d), gains, 0, tm=tm)

    half = QK_ROPE_DIM // 2
    inv = ROPE_THETA ** (-jnp.arange(half, dtype=F32) / half)
    pos = jnp.concatenate([jnp.arange(s_p), jnp.full((1,), past_len, jnp.int32)])
    ang = pos.astype(F32)[:, None] * inv[None, :]

    def rope_table(t):
        t = _pad_lanes(jnp.concatenate([t, t], axis=-1))
        return jnp.concatenate([jnp.tile(t[:s_p], (b_p, 1)), jnp.tile(t[s_p:], (b_s, 1))], axis=0)

    cos_tab = rope_table(jnp.cos(ang))
    sin_tab = rope_table(jnp.sin(ang))

    outs_conv = []
    for i in range(depth):
        kind = i % N_MIXERS
        g_post, g_ffn = 4 * i + 1, 4 * i + 2
        if kind == 0:
            j = i // N_MIXERS
            nb = d // 512
            (u,) = multi_mm("conf_in", h, [(ca_w_in, j, 0), (ca_w_in, j, nb)],
                            [(ca_b_in3, "col", j, 0), (ca_b_in3, "col", j, nb)],
                            _comb_glu, [F32], n_cols=d, tm=tm, tn=512)
            v = conf_conv_prompt(u, ca_dw, ca_dw_b3, ca_ln_g3, ca_ln_b3, j, batch=b_p, seq=s_p, ts=256)
            state_t = jnp.transpose(conv_state_in[j], (1, 0, 2))
            v, new_state_t = conf_conv_sample(v, state_t, u, ca_dw, ca_dw_b3, ca_ln_g3, ca_ln_b3, j,
                                              first_row=m_p, tb=16)
            x, h = out_proj("conf_out", v, ca_w_out, j, ca_b_out3, x, gains, g_post, g_ffn,
                            tm=OUT_PROJ_TILE)
            conv_p = _last_rows(u, b_p, s_p, ca_dw.shape[1] - 1)
            conv_s = jnp.transpose(new_state_t, (1, 0, 2))
            outs_conv.append((conv_p, conv_s))
        elif kind == 1:
            r = KV_LORA_RANK
            qr = mla_w_dq.shape[1]
            w_down = jnp.concatenate([mla_w_dq, mla_w_dkv[:, :r]], axis=1)[None]
            g_down = jnp.concatenate([mla_q_norm, mla_kv_norm])[None, None]
            lat_f32, lat_bf = multi_mm("mla_down", h, [(w_down, 0, 0)], [(g_down, "col", 0, 0)],
                                       _comb_rms, [F32, BF16], n_cols=qr + r, tm=tm, tn=512)
            w_kpe = mla_w_dkv[:, r:]
            kpe, kpe_bf = multi_mm("mla_kpe", h,
                                   [(_pad_lanes(w_kpe)[None], 0, 0), (_pad_lanes(_rot_cols(w_kpe))[None], 0, 0)],
                                   [(cos_tab, "row", None, None), (sin_tab, "row", None, None)],
                                   _comb_rope_f32_bf16, [F32, BF16], n_cols=LANES, tm=tm, tn=LANES)
            w_uq3 = mla_w_uq.reshape(qr, N_HEADS, QK_NOPE_DIM + QK_ROPE_DIM)
            w_qn = w_uq3[:, :, :QK_NOPE_DIM].reshape(1, qr, N_HEADS * QK_NOPE_DIM)
            w_qp = w_uq3[:, :, QK_NOPE_DIM:]
            w_qp_a = _pad_lanes(w_qp).reshape(1, qr, N_HEADS * LANES)
            w_qp_b = _pad_lanes(_rot_cols(w_qp)).reshape(1, qr, N_HEADS * LANES)
            (q_nope,) = multi_mm("mla_qn", lat_bf, [(w_qn, 0, 0)], [],
                                 functools.partial(_comb_scale, Q_SCALE), [BF16],
                                 n_cols=N_HEADS * QK_NOPE_DIM, tm=tm, tn=1024, x_cols=(0, qr))
            (q_pe,) = multi_mm("mla_qp", lat_bf, [(w_qp_a, 0, 0), (w_qp_b, 0, 0)],
                               [(cos_tab, "row", None, None), (sin_tab, "row", None, None)],
                               functools.partial(_comb_rope, Q_SCALE), [BF16],
                               n_cols=N_HEADS * LANES, tm=tm, tn=1024, x_cols=(0, qr))
            w_uk2 = mla_w_uk.reshape(r, N_HEADS * QK_NOPE_DIM)
            w_uv2 = mla_w_uv.reshape(r, N_HEADS * V_HEAD_DIM)
            w_up = jnp.concatenate([w_uk2, w_uv2], axis=1)[None]
            (kv_up,) = multi_mm("mla_kv_up", lat_bf, [(w_up, 0, 0)], [], _comb_identity, [BF16],
                                n_cols=w_up.shape[2], tm=tm, tn=1024, x_cols=(qr // r, r))
            att = flash_prompt(q_nope, q_pe, kv_up, kpe_bf, batch=b_p, seq=s_p)
            q_lat = absorb_q(q_nope, w_uk2, row_block=sample_block, n_rows=b_s).transpose(1, 0, 2)
            q_pe_s = q_pe[m_p:].reshape(b_s, N_HEADS, LANES)[:, :, :QK_ROPE_DIM]
            ckv_s = lat_f32[m_p:, qr:].reshape(b_s, 1, r)
            kpe_s = kpe[m_p:, :QK_ROPE_DIM].reshape(b_s, 1, QK_ROPE_DIM)
            cache_kpe_t = jnp.swapaxes(cache_kpe_l1, 1, 2)
            o_lat = decode_attention(page_table, q_lat, q_pe_s, ckv_s, kpe_s, cache_ckv_l1, cache_kpe_t)
            att = expand_o(att, o_lat.transpose(1, 0, 2), w_uv2, row_block=sample_block)
            x, h = out_proj("mla_out", att, mla_w_o[None], 0, None, x, gains, g_post, g_ffn,
                            tm=OUT_PROJ_TILE)
            ckv_p = lat_f32[:m_p, qr:].reshape(b_p, s_p, r)
            kpe_p = kpe[:m_p, :QK_ROPE_DIM].reshape(b_p, s_p, QK_ROPE_DIM)
        else:
            nb = d // 512
            sc_w_in3 = sc_w_in[None]
            gate, u = multi_mm("sc_in", h, [(sc_w_in3, 0, 0), (sc_w_in3, 0, nb), (sc_w_in3, 0, 2 * nb)], [],
                               _comb_short_conv_in, [F32, F32], n_cols=d, tm=tm, tn=512)
            v = short_conv_prompt(u, gate, sc_w_conv, batch=b_p, seq=s_p, ts=256)
            v = short_conv_sample(v, state_sconv_l2, u, gate, sc_w_conv, first_row=m_p, tb=32)
            x, h = out_proj("sc_out", v, sc_w_out[None], 0, None, x, gains, g_post, g_ffn,
                            tm=OUT_PROJ_TILE)
            n_keep = sc_w_conv.shape[0] - 1
            u_s = u[m_p:]
            sc_p = _last_rows(u, b_p, s_p, n_keep)
            sc_s = jnp.concatenate([state_sconv_l2[:, 1:], u_s[:, None, :]], axis=1)

        g_next = 4 * (i + 1) if i + 1 < depth else None
        x, h = ffn(h, x, ffn_w_gu, ffn_w_down, i, gains, 4 * i + 3, g_next,
                   tm=tm, tf=FFN_CHUNK, n_tail=b_s)

    y_prompt = x.reshape(b_p, s_p, d)
    y_sample = h.reshape(b_s, 1, d)
    return (y_prompt, y_sample, outs_conv[0][0], outs_conv[0][1], ckv_p, kpe_p,
            ckv_s, kpe_s, sc_p, sc_s, outs_conv[1][0], outs_conv[1][1])
```

```python
import functools

import jax
import jax.numpy as jnp
from jax import lax
from jax.experimental import pallas as pl
from jax.experimental.pallas import tpu as pltpu

F32 = jnp.float32
BF16 = jnp.bfloat16

N_HEADS = 16
QK_NOPE_DIM = 128
QK_ROPE_DIM = 64
V_HEAD_DIM = 128
KV_LORA_RANK = 512
ROPE_THETA = 10000.0
ATTN_SCALE = (QK_NOPE_DIM + QK_ROPE_DIM) ** -0.5
LOG2_E = 1.4426950408889634
Q_SCALE = ATTN_SCALE * LOG2_E
NORM_EPS = 1e-6
N_MIXERS = 3
LANES = 128

VMEM_LIMIT_BYTES = 58 * 1024 * 1024

ROW_TILE = 1040
OUT_PROJ_TILE = 416
OUT_PROJ_SPLIT = 2
FFN_CHUNK = 256

_SINGLE = pl.Buffered(1)
_NT = (((1,), (1,)), ((), ()))
_TN = (((0,), (0,)), ((), ()))


def _params(n_grid_axes):
    return pltpu.CompilerParams(
        dimension_semantics=("arbitrary",) * n_grid_axes,
        vmem_limit_bytes=VMEM_LIMIT_BYTES)


def _sigmoid(x):
    return 1.0 / (1.0 + jnp.exp(-x))


def _rms(x, g):
    return x * lax.rsqrt(jnp.mean(x * x, axis=-1, keepdims=True) + NORM_EPS) * g


def _layer_norm_silu(v, g, beta):
    mu = jnp.mean(v, axis=-1, keepdims=True)
    var = jnp.mean(jnp.square(v - mu), axis=-1, keepdims=True)
    y = (v - mu) * lax.rsqrt(var + NORM_EPS) * g + beta
    return y * _sigmoid(y)


def _stack_kernel(n_tail, xp_ref, xs_ref, g_ref, x_ref, h_ref):
    x_ref[...] = xp_ref[...]

    @pl.when(pl.program_id(0) == pl.num_programs(0) - 1)
    def _():
        x_ref[x_ref.shape[0] - n_tail:, :] = xs_ref[...]

    x = x_ref[...]
    h_ref[...] = _rms(x, g_ref[...]).astype(h_ref.dtype)


def stack_and_norm(xp, xs, gains, g_idx, *, tm):
    m_p, d = xp.shape
    m_s = xs.shape[0]
    m = m_p + m_s
    assert m % tm == 0 and m - m_p == m_s and tm - m_s >= 0 and (m_p % tm) == tm - m_s
    return pl.pallas_call(
        functools.partial(_stack_kernel, m_s),
        grid=(m // tm,),
        in_specs=[pl.BlockSpec((tm, d), lambda i: (i, 0)),
                  pl.BlockSpec((m_s, d), lambda i: (0, 0)),
                  pl.BlockSpec((None, 1, d), lambda i: (g_idx, 0, 0))],
        out_specs=[pl.BlockSpec((tm, d), lambda i: (i, 0)), pl.BlockSpec((tm, d), lambda i: (i, 0))],
        out_shape=[jax.ShapeDtypeStruct((m, d), F32), jax.ShapeDtypeStruct((m, d), BF16)],
        compiler_params=_params(1), name="stack_and_norm")(xp, xs, gains)


def _mm_kernel(n_w, n_ex, combine, x_ref, *refs):
    w_refs = refs[:n_w]
    ex_refs = refs[n_w:n_w + n_ex]
    out_refs = refs[n_w + n_ex:len(refs) - n_w]
    wb_refs = refs[len(refs) - n_w:]

    @pl.when(pl.program_id(1) == 0)
    def _():
        for w, wb in zip(w_refs, wb_refs):
            wb[...] = w[...].astype(BF16)

    x = x_ref[...]
    accs = [jnp.dot(x, wb[...], preferred_element_type=F32) for wb in wb_refs]
    outs = combine(accs, [e[...] for e in ex_refs])
    for o_ref, o in zip(out_refs, outs):
        o_ref[...] = o.astype(o_ref.dtype)


def multi_mm(name, x, ws, extras, combine, out_dtypes, *, n_cols, tm, tn, x_cols=None):
    m = x.shape[0]
    x_blk, k = (0, x.shape[1]) if x_cols is None else x_cols
    in_specs = [pl.BlockSpec((tm, k), lambda j, i: (i, x_blk))]
    args = [x]
    for w, layer, off in ws:
        in_specs.append(pl.BlockSpec((None, k, tn), lambda j, i, layer=layer, off=off: (layer, 0, j + off)))
        args.append(w)
    for arr, kind, layer, off in extras:
        if kind == "row":
            in_specs.append(pl.BlockSpec((tm, arr.shape[1]), lambda j, i: (i, 0)))
        else:
            in_specs.append(pl.BlockSpec((None, 1, tn), lambda j, i, layer=layer, off=off: (layer, 0, j + off)))
        args.append(arr)
    out_shape = [jax.ShapeDtypeStruct((m, n_cols), dt) for dt in out_dtypes]
    out_specs = [pl.BlockSpec((tm, tn), lambda j, i: (i, j)) for _ in out_dtypes]
    return pl.pallas_call(
        functools.partial(_mm_kernel, len(ws), len(extras), combine),
        grid=(n_cols // tn, m // tm), in_specs=in_specs, out_specs=out_specs, out_shape=out_shape,
        scratch_shapes=[pltpu.VMEM((k, tn), BF16) for _ in ws],
        compiler_params=_params(2), name=name)(*args)


def _out_proj_kernel(has_bias, v_ref, w_ref, *refs):
    if has_bias:
        b_ref, refs = refs[0], refs[1:]
    x_ref, g1_ref, g2_ref, xo_ref, h_ref, wb_ref = refs

    @pl.when(pl.program_id(0) == 0)
    def _():
        wb_ref[...] = w_ref[...].astype(BF16)

    rows = v_ref.shape[0] // OUT_PROJ_SPLIT
    for s in range(OUT_PROJ_SPLIT):
        sl = slice(s * rows, (s + 1) * rows)
        y = jnp.dot(v_ref[sl, :], wb_ref[...], preferred_element_type=F32)
        if has_bias:
            y = y + b_ref[...]
        x_new = x_ref[sl, :] + _rms(y, g1_ref[...])
        xo_ref[sl, :] = x_new
        h_ref[sl, :] = _rms(x_new, g2_ref[...]).astype(h_ref.dtype)


def out_proj(name, v, w, layer, bias, x, gains, g_post, g_next, *, tm):
    m, k = v.shape
    d = w.shape[2]
    in_specs = [pl.BlockSpec((tm, k), lambda i: (i, 0)),
                pl.BlockSpec((None, k, d), lambda i: (layer, 0, 0), pipeline_mode=_SINGLE)]
    args = [v, w]
    if bias is not None:
        in_specs.append(pl.BlockSpec((None, 1, d), lambda i: (layer, 0, 0)))
        args.append(bias)
    in_specs += [pl.BlockSpec((tm, d), lambda i: (i, 0)),
                 pl.BlockSpec((None, 1, d), lambda i: (g_post, 0, 0)),
                 pl.BlockSpec((None, 1, d), lambda i: (g_next, 0, 0))]
    args += [x, gains, gains]
    return pl.pallas_call(
        functools.partial(_out_proj_kernel, bias is not None),
        grid=(m // tm,), in_specs=in_specs,
        out_specs=[pl.BlockSpec((tm, d), lambda i: (i, 0)), pl.BlockSpec((tm, d), lambda i: (i, 0))],
        out_shape=[jax.ShapeDtypeStruct((m, d), F32), jax.ShapeDtypeStruct((m, d), BF16)],
        scratch_shapes=[pltpu.VMEM((k, d), BF16)],
        compiler_params=_params(1), name=name)(*args)


def _ffn_partial(h_ref, wg_ref, wu_ref, wd_ref):
    h = h_ref[...]
    g = jnp.dot(h, wg_ref[...].astype(BF16), preferred_element_type=F32)
    u = jnp.dot(h, wu_ref[...].astype(BF16), preferred_element_type=F32)
    act = (g * _sigmoid(g) * u).astype(BF16)
    return jnp.dot(act, wd_ref[...].astype(BF16), preferred_element_type=F32)


def _ffn_kernel(h_ref, x_ref, wg_ref, wu_ref, wd_ref, g1_ref, g2_ref, xo_ref, hn_ref):
    f = pl.program_id(1)

    @pl.when(f == 0)
    def _():
        xo_ref[...] = jnp.zeros(xo_ref.shape, F32)

    xo_ref[...] += _ffn_partial(h_ref, wg_ref, wu_ref, wd_ref)

    @pl.when(f == pl.num_programs(1) - 1)
    def _():
        x_new = x_ref[...] + _rms(xo_ref[...], g1_ref[...])
        xo_ref[...] = x_new
        hn_ref[...] = _rms(x_new, g2_ref[...]).astype(hn_ref.dtype)


def _ffn_last_kernel(n_tail, h_ref, x_ref, wg_ref, wu_ref, wd_ref, g1_ref, yp_ref, ys_ref):
    f = pl.program_id(1)

    @pl.when(f == 0)
    def _():
        yp_ref[...] = jnp.zeros(yp_ref.shape, F32)

    yp_ref[...] += _ffn_partial(h_ref, wg_ref, wu_ref, wd_ref)

    @pl.when(f == pl.num_programs(1) - 1)
    def _():
        yp_ref[...] = x_ref[...] + _rms(yp_ref[...], g1_ref[...])

        @pl.when(pl.program_id(0) == pl.num_programs(0) - 1)
        def _():
            ys_ref[...] = yp_ref[yp_ref.shape[0] - n_tail:, :]


def ffn(h, x, w_gu, w_down, layer, gains, g_post, g_next, *, tm, tf, n_tail):
    m, d = h.shape
    hidden = w_down.shape[1]
    n_f = hidden // tf
    in_specs = [pl.BlockSpec((tm, d), lambda i, f: (i, 0)),
                pl.BlockSpec((tm, d), lambda i, f: (i, 0)),
                pl.BlockSpec((None, d, tf), lambda i, f: (layer, 0, f)),
                pl.BlockSpec((None, d, tf), lambda i, f: (layer, 0, f + n_f)),
                pl.BlockSpec((None, tf, d), lambda i, f: (layer, f, 0)),
                pl.BlockSpec((None, 1, d), lambda i, f: (g_post, 0, 0))]
    args = [h, x, w_gu, w_gu, w_down, gains]
    row_spec = lambda: pl.BlockSpec((tm, d), lambda i, f: (i, 0), pipeline_mode=_SINGLE)
    if g_next is not None:
        in_specs.append(pl.BlockSpec((None, 1, d), lambda i, f: (g_next, 0, 0)))
        args.append(gains)
        return pl.pallas_call(
            _ffn_kernel, grid=(m // tm, n_f), in_specs=in_specs,
            out_specs=[row_spec(), row_spec()],
            out_shape=[jax.ShapeDtypeStruct((m, d), F32), jax.ShapeDtypeStruct((m, d), BF16)],
            compiler_params=_params(2), name="ffn")(*args)
    assert (m - n_tail) % tm == tm - n_tail
    return pl.pallas_call(
        functools.partial(_ffn_last_kernel, n_tail), grid=(m // tm, n_f), in_specs=in_specs,
        out_specs=[row_spec(), pl.BlockSpec((n_tail, d), lambda i, f: (0, 0))],
        out_shape=[jax.ShapeDtypeStruct((m - n_tail, d), F32), jax.ShapeDtypeStruct((n_tail, d), F32)],
        compiler_params=_params(2), name="ffn_last")(*args)


CONV_HALO = 32
CONV_ROWS = 128
SUBLANES = 8


def _conf_conv_kernel(ts, n_taps, halo_ref, main_ref, w_ref, b_ref, g_ref, beta_ref, o_ref,
                      win_ref, conv_ref, w8_ref):
    t = pl.program_id(1)
    d = main_ref.shape[1]
    n_c = d // LANES
    n_sub = CONV_ROWS // SUBLANES
    col_slices = [slice(c * LANES, (c + 1) * LANES) for c in range(n_c)]

    @pl.when((pl.program_id(0) == 0) & (t == 0))
    def _():
        for k in range(n_taps):
            for c, cols in enumerate(col_slices):
                w8_ref[k, c] = jnp.broadcast_to(w_ref[k:k + 1, cols], (SUBLANES, LANES))

    for c, cols in enumerate(col_slices):
        win_ref[c, 0:CONV_HALO, :] = jnp.where(t == 0, 0.0, halo_ref[:, cols])
        win_ref[c, CONV_HALO:, :] = main_ref[:, cols]
    first = CONV_HALO - (n_taps - 1)

    def column(c, carry):
        def row_chunk(rc, carry2):
            r0 = pl.multiple_of(rc * CONV_ROWS, CONV_ROWS)
            accs = [None] * n_sub
            for k in range(n_taps):
                src = win_ref[c, pl.ds(r0 + first + k, CONV_ROWS), :]
                wk = w8_ref[k, c]
                for rr in range(n_sub):
                    term = wk * src[rr * SUBLANES:(rr + 1) * SUBLANES, :]
                    accs[rr] = term if k == 0 else accs[rr] + term
            conv_ref[c, pl.ds(r0, CONV_ROWS), :] = jnp.concatenate(accs, axis=0)
            return carry2

        return lax.fori_loop(0, ts // CONV_ROWS, row_chunk, carry)

    lax.fori_loop(0, n_c, column, 0)

    vs = [conv_ref[c] + b_ref[:, cols] for c, cols in enumerate(col_slices)]
    mu = jnp.sum(functools.reduce(jnp.add, vs), axis=-1, keepdims=True) / d
    sq = functools.reduce(jnp.add, [jnp.square(v - mu) for v in vs])
    rstd = lax.rsqrt(jnp.sum(sq, axis=-1, keepdims=True) / d + NORM_EPS)
    for v, cols in zip(vs, col_slices):
        y = (v - mu) * rstd * g_ref[:, cols] + beta_ref[:, cols]
        o_ref[:, cols] = (y * _sigmoid(y)).astype(o_ref.dtype)


def conf_conv_prompt(u, dw, dw_b, ln_g, ln_b, layer, *, batch, seq, ts):
    m, d = u.shape
    n_taps = dw.shape[1]
    n_t = seq // ts
    halo_per_tile = ts // CONV_HALO
    vec = lambda: pl.BlockSpec((None, 1, d), lambda b, t: (layer, 0, 0))
    return pl.pallas_call(
        functools.partial(_conf_conv_kernel, ts, n_taps),
        grid=(batch, n_t),
        in_specs=[
            pl.BlockSpec((CONV_HALO, d),
                         lambda b, t: (jnp.maximum((b * n_t + t) * halo_per_tile - 1, 0), 0)),
            pl.BlockSpec((ts, d), lambda b, t: (b * n_t + t, 0)),
            pl.BlockSpec((None, n_taps, d), lambda b, t: (layer, 0, 0)),
            vec(), vec(), vec()],
        out_specs=pl.BlockSpec((ts, d), lambda b, t: (b * n_t + t, 0)),
        out_shape=jax.ShapeDtypeStruct((m, d), BF16),
        scratch_shapes=[pltpu.VMEM((d // LANES, ts + CONV_HALO, LANES), F32),
                        pltpu.VMEM((d // LANES, ts, LANES), F32),
                        pltpu.VMEM((n_taps, d // LANES, SUBLANES, LANES), F32)],
        compiler_params=_params(2), name="conf_conv_prompt")(u, u, dw, dw_b, ln_g, ln_b)


SC_HALO = 8


def _short_conv_kernel(ts, n_taps, halo_ref, main_ref, gate_ref, w_ref, o_ref, win_ref):
    t = pl.program_id(1)
    d = main_ref.shape[1]
    first = SC_HALO - (n_taps - 1)
    for c in range(d // LANES):
        cols = slice(c * LANES, (c + 1) * LANES)
        win_ref[c, 0:SC_HALO, :] = jnp.where(t == 0, 0.0, halo_ref[:, cols])
        win_ref[c, SC_HALO:, :] = main_ref[:, cols]
    for c in range(d // LANES):
        cols = slice(c * LANES, (c + 1) * LANES)
        acc = w_ref[0:1, cols] * win_ref[c, first:first + ts, :]
        for k in range(1, n_taps):
            acc = acc + w_ref[k:k + 1, cols] * win_ref[c, first + k:first + k + ts, :]
        o_ref[:, cols] = (gate_ref[:, cols] * acc).astype(o_ref.dtype)


def short_conv_prompt(u, gate, w_conv, *, batch, seq, ts):
    m, d = u.shape
    n_taps = w_conv.shape[0]
    n_t = seq // ts
    halo_per_tile = ts // SC_HALO
    return pl.pallas_call(
        functools.partial(_short_conv_kernel, ts, n_taps),
        grid=(batch, n_t),
        in_specs=[
            pl.BlockSpec((SC_HALO, d),
                         lambda b, t: (jnp.maximum((b * n_t + t) * halo_per_tile - 1, 0), 0)),
            pl.BlockSpec((ts, d), lambda b, t: (b * n_t + t, 0)),
            pl.BlockSpec((ts, d), lambda b, t: (b * n_t + t, 0)),
            pl.BlockSpec((n_taps, d), lambda b, t: (0, 0))],
        out_specs=pl.BlockSpec((ts, d), lambda b, t: (b * n_t + t, 0)),
        out_shape=jax.ShapeDtypeStruct((m, d), BF16),
        scratch_shapes=[pltpu.VMEM((d // LANES, ts + SC_HALO, LANES), F32)],
        compiler_params=_params(2), name="short_conv_prompt")(u, u, gate, w_conv)


def _conf_conv_sample_kernel(n_hist, buf_ref, st_ref, u_ref, w_ref, b_ref, g_ref, beta_ref,
                             o_ref, ns_ref):
    del buf_ref
    u = u_ref[...]
    acc = w_ref[n_hist:n_hist + 1, :] * u + b_ref[...]
    for k in range(n_hist):
        acc = acc + w_ref[k:k + 1, :] * st_ref[k]
    o_ref[...] = _layer_norm_silu(acc, g_ref[...], beta_ref[...]).astype(o_ref.dtype)
    ns_ref[0:n_hist - 1] = st_ref[1:n_hist]
    ns_ref[n_hist - 1] = u


def conf_conv_sample(buf, state_t, u, dw, dw_b, ln_g, ln_b, layer, *, first_row, tb):
    n_hist, n_b, d = state_t.shape
    blk0 = first_row // tb
    vec = lambda: pl.BlockSpec((None, 1, d), lambda i: (layer, 0, 0))
    return pl.pallas_call(
        functools.partial(_conf_conv_sample_kernel, n_hist),
        grid=(n_b // tb,),
        in_specs=[pl.BlockSpec(memory_space=pl.ANY),
                  pl.BlockSpec((n_hist, tb, d), lambda i: (0, i, 0)),
                  pl.BlockSpec((tb, d), lambda i: (blk0 + i, 0)),
                  pl.BlockSpec((None, n_hist + 1, d), lambda i: (layer, 0, 0)),
                  vec(), vec(), vec()],
        out_specs=[pl.BlockSpec((tb, d), lambda i: (blk0 + i, 0)),
                   pl.BlockSpec((n_hist, tb, d), lambda i: (0, i, 0))],
        out_shape=[jax.ShapeDtypeStruct(buf.shape, buf.dtype),
                   jax.ShapeDtypeStruct(state_t.shape, state_t.dtype)],
        input_output_aliases={0: 0},
        compiler_params=_params(1), name="conf_conv_sample")(buf, state_t, u, dw, dw_b, ln_g, ln_b)


def _short_conv_sample_kernel(n_hist, buf_ref, st_ref, u_ref, gate_ref, w_ref, o_ref):
    del buf_ref
    d = u_ref.shape[1]
    acc = w_ref[n_hist:n_hist + 1, :] * u_ref[...]
    for k in range(n_hist):
        acc = acc + w_ref[k:k + 1, :] * st_ref[:, k * d:(k + 1) * d]
    o_ref[...] = (gate_ref[...] * acc).astype(o_ref.dtype)


def short_conv_sample(buf, state, u, gate, w_conv, *, first_row, tb):
    n_b, n_hist, d = state.shape
    st2 = state.reshape(n_b, n_hist * d)
    blk0 = first_row // tb
    return pl.pallas_call(
        functools.partial(_short_conv_sample_kernel, n_hist),
        grid=(n_b // tb,),
        in_specs=[pl.BlockSpec(memory_space=pl.ANY),
                  pl.BlockSpec((tb, n_hist * d), lambda i: (i, 0)),
                  pl.BlockSpec((tb, d), lambda i: (blk0 + i, 0)),
                  pl.BlockSpec((tb, d), lambda i: (blk0 + i, 0)),
                  pl.BlockSpec((n_hist + 1, d), lambda i: (0, 0))],
        out_specs=pl.BlockSpec((tb, d), lambda i: (blk0 + i, 0)),
        out_shape=jax.ShapeDtypeStruct(buf.shape, buf.dtype),
        input_output_aliases={0: 0},
        compiler_params=_params(1), name="short_conv_sample")(buf, st2, u, gate, w_conv)


ATT_TQ = 512
ATT_TK = 512


def _flash_kernel(seq, qn_ref, qp_ref, kn_ref, kp_ref, v_ref, o_ref):
    k_full = jnp.concatenate([kn_ref[...], kp_ref[...]], axis=-1)
    n_q = seq // ATT_TQ
    qs = [jnp.concatenate([qn_ref[i * ATT_TQ:(i + 1) * ATT_TQ, :],
                           qp_ref[i * ATT_TQ:(i + 1) * ATT_TQ, :]], axis=-1) for i in range(n_q)]
    m_run = [jnp.full((1, ATT_TQ), -jnp.inf, F32) for _ in range(n_q)]
    l_run = [jnp.zeros((1, ATT_TQ), F32) for _ in range(n_q)]
    acc = [jnp.zeros((o_ref.shape[1], ATT_TQ), F32) for _ in range(n_q)]
    pairs = [(ki, qi) for ki in range(seq // ATT_TK) for qi in range(n_q)
             if (qi + 1) * ATT_TQ > ki * ATT_TK]

    def scores_t(ki, qi):
        k0 = ki * ATT_TK
        st = lax.dot_general(k_full[k0:k0 + ATT_TK, :], qs[qi], _NT, preferred_element_type=F32)
        if k0 + ATT_TK > qi * ATT_TQ:
            key = k0 + lax.broadcasted_iota(jnp.int32, st.shape, 0)
            qpos = qi * ATT_TQ + lax.broadcasted_iota(jnp.int32, st.shape, 1)
            st = jnp.where(key <= qpos, st, -jnp.inf)
        return st

    st_next = scores_t(*pairs[0])
    for t, (ki, qi) in enumerate(pairs):
        st = st_next
        if t + 1 < len(pairs):
            st_next = scores_t(*pairs[t + 1])
        m_new = jnp.maximum(m_run[qi], jnp.max(st, axis=0, keepdims=True))
        corr = jnp.exp2(m_run[qi] - m_new)
        p = jnp.exp2(st - m_new)
        l_run[qi] = l_run[qi] * corr + jnp.sum(p, axis=0, keepdims=True)
        m_run[qi] = m_new
        acc[qi] = acc[qi] * corr + lax.dot_general(
            v_ref[ki * ATT_TK:(ki + 1) * ATT_TK, :], p.astype(BF16), _TN, preferred_element_type=F32)
    for qi in range(n_q):
        o_ref[qi * ATT_TQ:(qi + 1) * ATT_TQ, :] = (acc[qi] / l_run[qi]).T.astype(o_ref.dtype)


def flash_prompt(q_nope, q_pe, kv_up, kpe, *, batch, seq):
    hd = V_HEAD_DIM
    spec = lambda col: pl.BlockSpec((seq, hd), col)
    return pl.pallas_call(
        functools.partial(_flash_kernel, seq),
        grid=(batch, N_HEADS),
        in_specs=[spec(lambda b, h: (b, h)), spec(lambda b, h: (b, h)),
                  spec(lambda b, h: (b, h)), spec(lambda b, h: (b, 0)),
                  spec(lambda b, h: (b, h + N_HEADS))],
        out_specs=spec(lambda b, h: (b, h)),
        out_shape=jax.ShapeDtypeStruct((q_nope.shape[0], N_HEADS * hd), BF16),
        compiler_params=_params(2), name="flash_prompt")(q_nope, q_pe, kv_up, kpe, kv_up)


PAGES_PER_STEP = 16
PAGE_RING = 3


def _per_head_nt_kernel(x_ref, w_ref, o_ref):
    o_ref[0] = lax.dot_general(x_ref[...], w_ref[...].astype(BF16), _NT,
                               preferred_element_type=F32).astype(o_ref.dtype)


def absorb_q(q_nope, w_uk2, *, row_block, n_rows):
    r = w_uk2.shape[0]
    return pl.pallas_call(
        _per_head_nt_kernel, grid=(N_HEADS,),
        in_specs=[pl.BlockSpec((n_rows, QK_NOPE_DIM), lambda h: (row_block, h)),
                  pl.BlockSpec((r, QK_NOPE_DIM), lambda h: (0, h))],
        out_specs=pl.BlockSpec((1, n_rows, r), lambda h: (h, 0, 0)),
        out_shape=jax.ShapeDtypeStruct((N_HEADS, n_rows, r), BF16),
        compiler_params=_params(1), name="absorb_q")(q_nope, w_uk2)


def _per_head_nn_kernel(buf_ref, x_ref, w_ref, o_ref):
    del buf_ref
    o_ref[...] = jnp.dot(x_ref[0], w_ref[...].astype(BF16),
                         preferred_element_type=F32).astype(o_ref.dtype)


def expand_o(buf, o_lat_t, w_uv2, *, row_block):
    n_h, n_b, r = o_lat_t.shape
    return pl.pallas_call(
        _per_head_nn_kernel, grid=(n_h,),
        in_specs=[pl.BlockSpec(memory_space=pl.ANY),
                  pl.BlockSpec((1, n_b, r), lambda h: (h, 0, 0)),
                  pl.BlockSpec((r, V_HEAD_DIM), lambda h: (0, h))],
        out_specs=pl.BlockSpec((n_b, V_HEAD_DIM), lambda h: (row_block, h)),
        out_shape=jax.ShapeDtypeStruct(buf.shape, buf.dtype),
        input_output_aliases={0: 0},
        compiler_params=_params(1), name="expand_o")(buf, o_lat_t, w_uv2)


def _page_copies(pt_ref, ckv_hbm, kpt_hbm, ck_ring, kp_ring, sem_ck, sem_kp, chunk, n_groups):
    slot = chunk % PAGE_RING
    b = chunk // n_groups
    g = chunk % n_groups
    copies = []
    for i in range(PAGES_PER_STEP):
        pg = pt_ref[b, g * PAGES_PER_STEP + i]
        copies.append(pltpu.make_async_copy(ckv_hbm.at[pg], ck_ring.at[slot, i], sem_ck.at[slot]))
        copies.append(pltpu.make_async_copy(kpt_hbm.at[pg], kp_ring.at[slot, i], sem_kp.at[slot]))
    return copies


def _decode_kernel(pt_ref, ql_ref, qp_ref, cn_ref, kn_ref, ckv_hbm, kpt_hbm, o_ref,
                   ck_ring, kp_ring, sem_ck, sem_kp, ck_all, s_all):
    g = pl.program_id(1)
    n_groups = pl.num_programs(1)
    n_chunks = pl.num_programs(0) * n_groups
    chunk = pl.program_id(0) * n_groups + g
    page = ck_ring.shape[2]
    ring_args = (pt_ref, ckv_hbm, kpt_hbm, ck_ring, kp_ring, sem_ck, sem_kp)

    @pl.when(chunk == 0)
    def _():
        for c in range(PAGE_RING - 1):
            for cp in _page_copies(*ring_args, jnp.int32(c), n_groups):
                cp.start()

    @pl.when(chunk + (PAGE_RING - 1) < n_chunks)
    def _():
        for cp in _page_copies(*ring_args, chunk + (PAGE_RING - 1), n_groups):
            cp.start()

    for cp in _page_copies(*ring_args, chunk, n_groups):
        cp.wait()

    slot = chunk % PAGE_RING
    q = ql_ref[0]
    qp = qp_ref[0]
    for i in range(PAGES_PER_STEP):
        ck_page = ck_ring[slot, i].astype(BF16)
        ck_all[g, i * page:(i + 1) * page, :] = ck_page
        s_all[g, :, i * page:(i + 1) * page] = (
            lax.dot_general(q, ck_page, _NT, preferred_element_type=F32)
            + jnp.dot(qp, kp_ring[slot, i].astype(BF16), preferred_element_type=F32))

    @pl.when(g == n_groups - 1)
    def _():
        n_g = s_all.shape[0]
        c_new = cn_ref[0].astype(BF16).astype(F32)
        k_new = kn_ref[0].astype(BF16).astype(F32)
        s_self = (jnp.sum(q.astype(F32) * c_new, axis=-1, keepdims=True)
                  + jnp.sum(qp.astype(F32) * k_new, axis=-1, keepdims=True))
        m_fin = s_self
        for j in range(n_g):
            m_fin = jnp.maximum(m_fin, jnp.max(s_all[j], axis=-1, keepdims=True))
        p_self = jnp.exp2(s_self - m_fin)
        l_fin = p_self
        acc = p_self.astype(BF16).astype(F32) * c_new
        for j in range(n_g):
            p = jnp.exp2(s_all[j] - m_fin)
            l_fin = l_fin + jnp.sum(p, axis=-1, keepdims=True)
            acc = acc + jnp.dot(p.astype(BF16), ck_all[j], preferred_element_type=F32)
        o_ref[0] = (acc / l_fin).astype(o_ref.dtype)


def decode_attention(page_table, q_lat, q_pe, c_new, kpe_new, cache_ckv, cache_kpe_t):
    n_b, n_h, r = q_lat.shape
    p_dim = q_pe.shape[2]
    page = cache_ckv.shape[1]
    n_pages = page_table.shape[1]
    g_pages = PAGES_PER_STEP
    n_groups = n_pages // g_pages
    assert n_groups * g_pages == n_pages and n_b * n_groups >= PAGE_RING - 1

    in_specs = [pl.BlockSpec((1, n_h, r), lambda b, g, pt: (b, 0, 0)),
                pl.BlockSpec((1, n_h, p_dim), lambda b, g, pt: (b, 0, 0)),
                pl.BlockSpec((1, 1, r), lambda b, g, pt: (b, 0, 0)),
                pl.BlockSpec((1, 1, p_dim), lambda b, g, pt: (b, 0, 0)),
                pl.BlockSpec(memory_space=pl.ANY),
                pl.BlockSpec(memory_space=pl.ANY)]
    grid_spec = pltpu.PrefetchScalarGridSpec(
        num_scalar_prefetch=1, grid=(n_b, n_groups), in_specs=in_specs,
        out_specs=pl.BlockSpec((1, n_h, r), lambda b, g, pt: (b, 0, 0)),
        scratch_shapes=[pltpu.VMEM((PAGE_RING, g_pages, page, r), F32),
                        pltpu.VMEM((PAGE_RING, g_pages, p_dim, page), F32),
                        pltpu.SemaphoreType.DMA((PAGE_RING,)),
                        pltpu.SemaphoreType.DMA((PAGE_RING,)),
                        pltpu.VMEM((n_groups, g_pages * page, r), BF16),
                        pltpu.VMEM((n_groups, n_h, g_pages * page), F32)])
    return pl.pallas_call(
        _decode_kernel, grid_spec=grid_spec,
        out_shape=jax.ShapeDtypeStruct((n_b, n_h, r), BF16),
        compiler_params=_params(2), name="decode_attention")(
            page_table, q_lat, q_pe, c_new, kpe_new, cache_ckv, cache_kpe_t)


def _comb_identity(accs, ex):
    return (accs[0],)


def _comb_glu(accs, ex):
    return ((accs[0] + ex[0]) * _sigmoid(accs[1] + ex[1]),)


def _comb_short_conv_in(accs, ex):
    return (accs[0], accs[1] * accs[2])


def _comb_rms(accs, ex):
    y = _rms(accs[0], ex[0])
    return (y, y)


def _tile_lanes(tab, width):
    reps = width // tab.shape[1]
    return tab if reps == 1 else jnp.concatenate([tab] * reps, axis=-1)


def _comb_rope(scale, accs, ex):
    cos_t = _tile_lanes(ex[0], accs[0].shape[1])
    sin_t = _tile_lanes(ex[1], accs[0].shape[1])
    return ((accs[0] * cos_t + accs[1] * sin_t) * scale,)


def _comb_rope_f32_bf16(accs, ex):
    y = accs[0] * ex[0] + accs[1] * ex[1]
    return (y, y)


def _comb_scale(scale, accs, ex):
    return (accs[0] * scale,)


def _rot_cols(w):
    half = w.shape[-1] // 2
    return jnp.concatenate([-w[..., half:], w[..., :half]], axis=-1)


def _last_rows(u, n_batch, seq, n_keep):
    return jnp.stack([u[(b + 1) * seq - n_keep:(b + 1) * seq] for b in range(n_batch)])


def _pad_lanes(w):
    pad = LANES - w.shape[-1]
    return jnp.concatenate([w, jnp.zeros(w.shape[:-1] + (pad,), w.dtype)], axis=-1)


def kernel(x_prompt, x_sample, state_conv_l0, cache_ckv_l1, cache_kpe_l1, state_sconv_l2, state_conv_l3, page_table, norm_g, ca_w_in, ca_b_in, ca_dw, ca_dw_b, ca_ln_g, ca_ln_b, ca_w_out, ca_b_out, mla_w_dq, mla_q_norm, mla_w_uq, mla_w_dkv, mla_kv_norm, mla_w_uk, mla_w_uv, mla_w_o, sc_w_in, sc_w_conv, sc_w_out, ffn_w_gu, ffn_w_down):
    b_p, s_p, d = x_prompt.shape
    b_s = x_sample.shape[0]
    m_p = b_p * s_p
    depth = norm_g.shape[0]
    past_len = page_table.shape[1] * cache_ckv_l1.shape[1]
    conv_state_in = (state_conv_l0, state_conv_l3)
    tm = ROW_TILE
    sample_block = m_p // b_s

    gains = norm_g.reshape(depth * 4, 1, d)
    ca_b_in3 = ca_b_in[:, None, :]
    ca_b_out3 = ca_b_out[:, None, :]
    ca_dw_b3 = ca_dw_b[:, None, :]
    ca_ln_g3 = ca_ln_g[:, None, :]
    ca_ln_b3 = ca_ln_b[:, None, :]

    x, h = stack_and_norm(x_prompt.reshape(m_p, d), x_sample.reshape(b_s, d), gains, 0, tm=tm)

    half = QK_ROPE_DIM // 2
    inv = ROPE_THETA ** (-jnp.arange(half, dtype=F32) / half)
    pos = jnp.concatenate([jnp.arange(s_p), jnp.full((1,), past_len, jnp.int32)])
    ang = pos.astype(F32)[:, None] * inv[None, :]

    def rope_table(t):
        t = _pad_lanes(jnp.concatenate([t, t], axis=-1))
        return jnp.concatenate([jnp.tile(t[:s_p], (b_p, 1)), jnp.tile(t[s_p:], (b_s, 1))], axis=0)

    cos_tab = rope_table(jnp.cos(ang))
    sin_tab = rope_table(jnp.sin(ang))

    outs_conv = []
    for i in range(depth):
        kind = i % N_MIXERS
        g_post, g_ffn = 4 * i + 1, 4 * i + 2
        if kind == 0:
            j = i // N_MIXERS
            nb = d // 512
            (u,) = multi_mm("conf_in", h, [(ca_w_in, j, 0), (ca_w_in, j, nb)],
                            [(ca_b_in3, "col", j, 0), (ca_b_in3, "col", j, nb)],
                            _comb_glu, [F32], n_cols=d, tm=tm, tn=512)
            v = conf_conv_prompt(u, ca_dw, ca_dw_b3, ca_ln_g3, ca_ln_b3, j, batch=b_p, seq=s_p, ts=256)
            state_t = jnp.transpose(conv_state_in[j], (1, 0, 2))
            v, new_state_t = conf_conv_sample(v, state_t, u, ca_dw, ca_dw_b3, ca_ln_g3, ca_ln_b3, j,
                                              first_row=m_p, tb=16)
            x, h = out_proj("conf_out", v, ca_w_out, j, ca_b_out3, x, gains, g_post, g_ffn,
                            tm=OUT_PROJ_TILE)
            conv_p = _last_rows(u, b_p, s_p, ca_dw.shape[1] - 1)
            conv_s = jnp.transpose(new_state_t, (1, 0, 2))
            outs_conv.append((conv_p, conv_s))
        elif kind == 1:
            r = KV_LORA_RANK
            qr = mla_w_dq.shape[1]
            w_down = jnp.concatenate([mla_w_dq, mla_w_dkv[:, :r]], axis=1)[None]
            g_down = jnp.concatenate([mla_q_norm, mla_kv_norm])[None, None]
            lat_f32, lat_bf = multi_mm("mla_down", h, [(w_down, 0, 0)], [(g_down, "col", 0, 0)],
                                       _comb_rms, [F32, BF16], n_cols=qr + r, tm=tm, tn=512)
            w_kpe = mla_w_dkv[:, r:]
            kpe, kpe_bf = multi_mm("mla_kpe", h,
                                   [(_pad_lanes(w_kpe)[None], 0, 0), (_pad_lanes(_rot_cols(w_kpe))[None], 0, 0)],
                                   [(cos_tab, "row", None, None), (sin_tab, "row", None, None)],
                                   _comb_rope_f32_bf16, [F32, BF16], n_cols=LANES, tm=tm, tn=LANES)
            w_uq3 = mla_w_uq.reshape(qr, N_HEADS, QK_NOPE_DIM + QK_ROPE_DIM)
            w_qn = w_uq3[:, :, :QK_NOPE_DIM].reshape(1, qr, N_HEADS * QK_NOPE_DIM)
            w_qp = w_uq3[:, :, QK_NOPE_DIM:]
            w_qp_a = _pad_lanes(w_qp).reshape(1, qr, N_HEADS * LANES)
            w_qp_b = _pad_lanes(_rot_cols(w_qp)).reshape(1, qr, N_HEADS * LANES)
            (q_nope,) = multi_mm("mla_qn", lat_bf, [(w_qn, 0, 0)], [],
                                 functools.partial(_comb_scale, Q_SCALE), [BF16],
                                 n_cols=N_HEADS * QK_NOPE_DIM, tm=tm, tn=1024, x_cols=(0, qr))
            (q_pe,) = multi_mm("mla_qp", lat_bf, [(w_qp_a, 0, 0), (w_qp_b, 0, 0)],
                               [(cos_tab, "row", None, None), (sin_tab, "row", None, None)],
                               functools.partial(_comb_rope, Q_SCALE), [BF16],
                               n_cols=N_HEADS * LANES, tm=tm, tn=1024, x_cols=(0, qr))
            w_uk2 = mla_w_uk.reshape(r, N_HEADS * QK_NOPE_DIM)
            w_uv2 = mla_w_uv.reshape(r, N_HEADS * V_HEAD_DIM)
            w_up = jnp.concatenate([w_uk2, w_uv2], axis=1)[None]
            (kv_up,) = multi_mm("mla_kv_up", lat_bf, [(w_up, 0, 0)], [], _comb_identity, [BF16],
                                n_cols=w_up.shape[2], tm=tm, tn=1024, x_cols=(qr // r, r))
            att = flash_prompt(q_nope, q_pe, kv_up, kpe_bf, batch=b_p, seq=s_p)
            q_lat = absorb_q(q_nope, w_uk2, row_block=sample_block, n_rows=b_s).transpose(1, 0, 2)
            q_pe_s = q_pe[m_p:].reshape(b_s, N_HEADS, LANES)[:, :, :QK_ROPE_DIM]
            ckv_s = lat_f32[m_p:, qr:].reshape(b_s, 1, r)
            kpe_s = kpe[m_p:, :QK_ROPE_DIM].reshape(b_s, 1, QK_ROPE_DIM)
            cache_kpe_t = jnp.swapaxes(cache_kpe_l1, 1, 2)
            o_lat = decode_attention(page_table, q_lat, q_pe_s, ckv_s, kpe_s, cache_ckv_l1, cache_kpe_t)
            att = expand_o(att, o_lat.transpose(1, 0, 2), w_uv2, row_block=sample_block)
            x, h = out_proj("mla_out", att, mla_w_o[None], 0, None, x, gains, g_post, g_ffn,
                            tm=OUT_PROJ_TILE)
            ckv_p = lat_f32[:m_p, qr:].reshape(b_p, s_p, r)
            kpe_p = kpe[:m_p, :QK_ROPE_DIM].reshape(b_p, s_p, QK_ROPE_DIM)
        else:
            nb = d // 512
            sc_w_in3 = sc_w_in[None]
            gate, u = multi_mm("sc_in", h, [(sc_w_in3, 0, 0), (sc_w_in3, 0, nb), (sc_w_in3, 0, 2 * nb)], [],
                               _comb_short_conv_in, [F32, F32], n_cols=d, tm=tm, tn=512)
            v = short_conv_prompt(u, gate, sc_w_conv, batch=b_p, seq=s_p, ts=256)
            v = short_conv_sample(v, state_sconv_l2, u, gate, sc_w_conv, first_row=m_p, tb=32)
            x, h = out_proj("sc_out", v, sc_w_out[None], 0, None, x, gains, g_post, g_ffn,
                            tm=OUT_PROJ_TILE)
            n_keep = sc_w_conv.shape[0] - 1
            u_s = u[m_p:]
            sc_p = _last_rows(u, b_p, s_p, n_keep)
            sc_s = jnp.concatenate([state_sconv_l2[:, 1:], u_s[:, None, :]], axis=1)

        g_next = 4 * (i + 1) if i + 1 < depth else None
        x, h = ffn(h, x, ffn_w_gu, ffn_w_down, i, gains, 4 * i + 3, g_next,
                   tm=tm, tf=FFN_CHUNK, n_tail=b_s)

    y_prompt = x.reshape(b_p, s_p, d)
    y_sample = h.reshape(b_s, 1, d)
    return (y_prompt, y_sample, outs_conv[0][0], outs_conv[0][1], ckv_p, kpe_p,
            ckv_s, kpe_s, sc_p, sc_s, outs_conv[1][0], outs_conv[1][1])
```

```python
import functools

import jax
import jax.numpy as jnp
from jax import lax
from jax.experimental import pallas as pl
from jax.experimental.pallas import tpu as pltpu

F32 = jnp.float32
BF16 = jnp.bfloat16

N_HEADS = 16
QK_NOPE_DIM = 128
QK_ROPE_DIM = 64
V_HEAD_DIM = 128
KV_LORA_RANK = 512
ROPE_THETA = 10000.0
ATTN_SCALE = (QK_NOPE_DIM + QK_ROPE_DIM) ** -0.5
LOG2_E = 1.4426950408889634
Q_SCALE = ATTN_SCALE * LOG2_E
NORM_EPS = 1e-6
N_MIXERS = 3
LANES = 128

VMEM_LIMIT_BYTES = 58 * 1024 * 1024

ROW_TILE = 1040
OUT_PROJ_TILE = 416
OUT_PROJ_SPLIT = 2
FFN_CHUNK = 256

_SINGLE = pl.Buffered(1)
_NT = (((1,), (1,)), ((), ()))
_TN = (((0,), (0,)), ((), ()))


def _params(n_grid_axes):
    return pltpu.CompilerParams(
        dimension_semantics=("arbitrary",) * n_grid_axes,
        vmem_limit_bytes=VMEM_LIMIT_BYTES)


def _sigmoid(x):
    return 1.0 / (1.0 + jnp.exp(-x))


def _rms(x, g):
    return x * lax.rsqrt(jnp.mean(x * x, axis=-1, keepdims=True) + NORM_EPS) * g


def _layer_norm_silu(v, g, beta):
    mu = jnp.mean(v, axis=-1, keepdims=True)
    var = jnp.mean(jnp.square(v - mu), axis=-1, keepdims=True)
    y = (v - mu) * lax.rsqrt(var + NORM_EPS) * g + beta
    return y * _sigmoid(y)


def _stack_kernel(n_tail, xp_ref, xs_ref, g_ref, x_ref, h_ref):
    x_ref[...] = xp_ref[...]

    @pl.when(pl.program_id(0) == pl.num_programs(0) - 1)
    def _():
        x_ref[x_ref.shape[0] - n_tail:, :] = xs_ref[...]

    x = x_ref[...]
    h_ref[...] = _rms(x, g_ref[...]).astype(h_ref.dtype)


def stack_and_norm(xp, xs, gains, g_idx, *, tm):
    m_p, d = xp.shape
    m_s = xs.shape[0]
    m = m_p + m_s
    assert m % tm == 0 and m - m_p == m_s and tm - m_s >= 0 and (m_p % tm) == tm - m_s
    return pl.pallas_call(
        functools.partial(_stack_kernel, m_s),
        grid=(m // tm,),
        in_specs=[pl.BlockSpec((tm, d), lambda i: (i, 0)),
                  pl.BlockSpec((m_s, d), lambda i: (0, 0)),
                  pl.BlockSpec((None, 1, d), lambda i: (g_idx, 0, 0))],
        out_specs=[pl.BlockSpec((tm, d), lambda i: (i, 0)), pl.BlockSpec((tm, d), lambda i: (i, 0))],
        out_shape=[jax.ShapeDtypeStruct((m, d), F32), jax.ShapeDtypeStruct((m, d), BF16)],
        compiler_params=_params(1), name="stack_and_norm")(xp, xs, gains)


def _mm_kernel(n_w, n_ex, combine, x_ref, *refs):
    w_refs = refs[:n_w]
    ex_refs = refs[n_w:n_w + n_ex]
    out_refs = refs[n_w + n_ex:len(refs) - n_w]
    wb_refs = refs[len(refs) - n_w:]

    @pl.when(pl.program_id(1) == 0)
    def _():
        for w, wb in zip(w_refs, wb_refs):
            wb[...] = w[...].astype(BF16)

    x = x_ref[...]
    accs = [jnp.dot(x, wb[...], preferred_element_type=F32) for wb in wb_refs]
    outs = combine(accs, [e[...] for e in ex_refs])
    for o_ref, o in zip(out_refs, outs):
        o_ref[...] = o.astype(o_ref.dtype)


def multi_mm(name, x, ws, extras, combine, out_dtypes, *, n_cols, tm, tn, x_cols=None):
    m = x.shape[0]
    x_blk, k = (0, x.shape[1]) if x_cols is None else x_cols
    in_specs = [pl.BlockSpec((tm, k), lambda j, i: (i, x_blk))]
    args = [x]
    for w, layer, off in ws:
        in_specs.append(pl.BlockSpec((None, k, tn), lambda j, i, layer=layer, off=off: (layer, 0, j + off)))
        args.append(w)
    for arr, kind, layer, off in extras:
        if kind == "row":
            in_specs.append(pl.BlockSpec((tm, arr.shape[1]), lambda j, i: (i, 0)))
        else:
            in_specs.append(pl.BlockSpec((None, 1, tn), lambda j, i, layer=layer, off=off: (layer, 0, j + off)))
        args.append(arr)
    out_shape = [jax.ShapeDtypeStruct((m, n_cols), dt) for dt in out_dtypes]
    out_specs = [pl.BlockSpec((tm, tn), lambda j, i: (i, j)) for _ in out_dtypes]
    return pl.pallas_call(
        functools.partial(_mm_kernel, len(ws), len(extras), combine),
        grid=(n_cols // tn, m // tm), in_specs=in_specs, out_specs=out_specs, out_shape=out_shape,
        scratch_shapes=[pltpu.VMEM((k, tn), BF16) for _ in ws],
        compiler_params=_params(2), name=name)(*args)


def _out_proj_kernel(has_bias, v_ref, w_ref, *refs):
    if has_bias:
        b_ref, refs = refs[0], refs[1:]
    x_ref, g1_ref, g2_ref, xo_ref, h_ref, wb_ref = refs

    @pl.when(pl.program_id(0) == 0)
    def _():
        wb_ref[...] = w_ref[...].astype(BF16)

    rows = v_ref.shape[0] // OUT_PROJ_SPLIT
    for s in range(OUT_PROJ_SPLIT):
        sl = slice(s * rows, (s + 1) * rows)
        y = jnp.dot(v_ref[sl, :], wb_ref[...], preferred_element_type=F32)
        if has_bias:
            y = y + b_ref[...]
        x_new = x_ref[sl, :] + _rms(y, g1_ref[...])
        xo_ref[sl, :] = x_new
        h_ref[sl, :] = _rms(x_new, g2_ref[...]).astype(h_ref.dtype)


def out_proj(name, v, w, layer, bias, x, gains, g_post, g_next, *, tm):
    m, k = v.shape
    d = w.shape[2]
    in_specs = [pl.BlockSpec((tm, k), lambda i: (i, 0)),
                pl.BlockSpec((None, k, d), lambda i: (layer, 0, 0), pipeline_mode=_SINGLE)]
    args = [v, w]
    if bias is not None:
        in_specs.append(pl.BlockSpec((None, 1, d), lambda i: (layer, 0, 0)))
        args.append(bias)
    in_specs += [pl.BlockSpec((tm, d), lambda i: (i, 0)),
                 pl.BlockSpec((None, 1, d), lambda i: (g_post, 0, 0)),
                 pl.BlockSpec((None, 1, d), lambda i: (g_next, 0, 0))]
    args += [x, gains, gains]
    return pl.pallas_call(
        functools.partial(_out_proj_kernel, bias is not None),
        grid=(m // tm,), in_specs=in_specs,
        out_specs=[pl.BlockSpec((tm, d), lambda i: (i, 0)), pl.BlockSpec((tm, d), lambda i: (i, 0))],
        out_shape=[jax.ShapeDtypeStruct((m, d), F32), jax.ShapeDtypeStruct((m, d), BF16)],
        scratch_shapes=[pltpu.VMEM((k, d), BF16)],
        compiler_params=_params(1), name=name)(*args)


def _ffn_partial(h_ref, wg_ref, wu_ref, wd_ref):
    h = h_ref[...]
    g = jnp.dot(h, wg_ref[...].astype(BF16), preferred_element_type=F32)
    u = jnp.dot(h, wu_ref[...].astype(BF16), preferred_element_type=F32)
    act = (g * _sigmoid(g) * u).astype(BF16)
    return jnp.dot(act, wd_ref[...].astype(BF16), preferred_element_type=F32)


def _ffn_kernel(h_ref, x_ref, wg_ref, wu_ref, wd_ref, g1_ref, g2_ref, xo_ref, hn_ref):
    f = pl.program_id(1)

    @pl.when(f == 0)
    def _():
        xo_ref[...] = jnp.zeros(xo_ref.shape, F32)

    xo_ref[...] += _ffn_partial(h_ref, wg_ref, wu_ref, wd_ref)

    @pl.when(f == pl.num_programs(1) - 1)
    def _():
        x_new = x_ref[...] + _rms(xo_ref[...], g1_ref[...])
        xo_ref[...] = x_new
        hn_ref[...] = _rms(x_new, g2_ref[...]).astype(hn_ref.dtype)


def _ffn_last_kernel(n_tail, h_ref, x_ref, wg_ref, wu_ref, wd_ref, g1_ref, yp_ref, ys_ref):
    f = pl.program_id(1)

    @pl.when(f == 0)
    def _():
        yp_ref[...] = jnp.zeros(yp_ref.shape, F32)

    yp_ref[...] += _ffn_partial(h_ref, wg_ref, wu_ref, wd_ref)

    @pl.when(f == pl.num_programs(1) - 1)
    def _():
        yp_ref[...] = x_ref[...] + _rms(yp_ref[...], g1_ref[...])

        @pl.when(pl.program_id(0) == pl.num_programs(0) - 1)
        def _():
            ys_ref[...] = yp_ref[yp_ref.shape[0] - n_tail:, :]


def ffn(h, x, w_gu, w_down, layer, gains, g_post, g_next, *, tm, tf, n_tail):
    m, d = h.shape
    hidden = w_down.shape[1]
    n_f = hidden // tf
    in_specs = [pl.BlockSpec((tm, d), lambda i, f: (i, 0)),
                pl.BlockSpec((tm, d), lambda i, f: (i, 0)),
                pl.BlockSpec((None, d, tf), lambda i, f: (layer, 0, f)),
                pl.BlockSpec((None, d, tf), lambda i, f: (layer, 0, f + n_f)),
                pl.BlockSpec((None, tf, d), lambda i, f: (layer, f, 0)),
                pl.BlockSpec((None, 1, d), lambda i, f: (g_post, 0, 0))]
    args = [h, x, w_gu, w_gu, w_down, gains]
    row_spec = lambda: pl.BlockSpec((tm, d), lambda i, f: (i, 0), pipeline_mode=_SINGLE)
    if g_next is not None:
        in_specs.append(pl.BlockSpec((None, 1, d), lambda i, f: (g_next, 0, 0)))
        args.append(gains)
        return pl.pallas_call(
            _ffn_kernel, grid=(m // tm, n_f), in_specs=in_specs,
            out_specs=[row_spec(), row_spec()],
            out_shape=[jax.ShapeDtypeStruct((m, d), F32), jax.ShapeDtypeStruct((m, d), BF16)],
            compiler_params=_params(2), name="ffn")(*args)
    assert (m - n_tail) % tm == tm - n_tail
    return pl.pallas_call(
        functools.partial(_ffn_last_kernel, n_tail), grid=(m // tm, n_f), in_specs=in_specs,
        out_specs=[row_spec(), pl.BlockSpec((n_tail, d), lambda i, f: (0, 0))],
        out_shape=[jax.ShapeDtypeStruct((m - n_tail, d), F32), jax.ShapeDtypeStruct((n_tail, d), F32)],
        compiler_params=_params(2), name="ffn_last")(*args)


CONV_HALO = 32
CONV_ROWS = 128
SUBLANES = 8


def _conf_conv_kernel(ts, n_taps, halo_ref, main_ref, w_ref, b_ref, g_ref, beta_ref, o_ref,
                      win_ref, conv_ref, w8_ref):
    t = pl.program_id(1)
    d = main_ref.shape[1]
    n_c = d // LANES
    n_sub = CONV_ROWS // SUBLANES
    col_slices = [slice(c * LANES, (c + 1) * LANES) for c in range(n_c)]

    @pl.when((pl.program_id(0) == 0) & (t == 0))
    def _():
        for k in range(n_taps):
            for c, cols in enumerate(col_slices):
                w8_ref[k, c] = jnp.broadcast_to(w_ref[k:k + 1, cols], (SUBLANES, LANES))

    for c, cols in enumerate(col_slices):
        win_ref[c, 0:CONV_HALO, :] = jnp.where(t == 0, 0.0, halo_ref[:, cols])
        win_ref[c, CONV_HALO:, :] = main_ref[:, cols]
    first = CONV_HALO - (n_taps - 1)

    def column(c, carry):
        def row_chunk(rc, carry2):
            r0 = pl.multiple_of(rc * CONV_ROWS, CONV_ROWS)
            accs = [None] * n_sub
            for k in range(n_taps):
                src = win_ref[c, pl.ds(r0 + first + k, CONV_ROWS), :]
                wk = w8_ref[k, c]
                for rr in range(n_sub):
                    term = wk * src[rr * SUBLANES:(rr + 1) * SUBLANES, :]
                    accs[rr] = term if k == 0 else accs[rr] + term
            conv_ref[c, pl.ds(r0, CONV_ROWS), :] = jnp.concatenate(accs, axis=0)
            return carry2

        return lax.fori_loop(0, ts // CONV_ROWS, row_chunk, carry)

    lax.fori_loop(0, n_c, column, 0)

    vs = [conv_ref[c] + b_ref[:, cols] for c, cols in enumerate(col_slices)]
    mu = jnp.sum(functools.reduce(jnp.add, vs), axis=-1, keepdims=True) / d
    sq = functools.reduce(jnp.add, [jnp.square(v - mu) for v in vs])
    rstd = lax.rsqrt(jnp.sum(sq, axis=-1, keepdims=True) / d + NORM_EPS)
    for v, cols in zip(vs, col_slices):
        y = (v - mu) * rstd * g_ref[:, cols] + beta_ref[:, cols]
        o_ref[:, cols] = (y * _sigmoid(y)).astype(o_ref.dtype)


def conf_conv_prompt(u, dw, dw_b, ln_g, ln_b, layer, *, batch, seq, ts):
    m, d = u.shape
    n_taps = dw.shape[1]
    n_t = seq // ts
    halo_per_tile = ts // CONV_HALO
    vec = lambda: pl.BlockSpec((None, 1, d), lambda b, t: (layer, 0, 0))
    return pl.pallas_call(
        functools.partial(_conf_conv_kernel, ts, n_taps),
        grid=(batch, n_t),
        in_specs=[
            pl.BlockSpec((CONV_HALO, d),
                         lambda b, t: (jnp.maximum((b * n_t + t) * halo_per_tile - 1, 0), 0)),
            pl.BlockSpec((ts, d), lambda b, t: (b * n_t + t, 0)),
            pl.BlockSpec((None, n_taps, d), lambda b, t: (layer, 0, 0)),
            vec(), vec(), vec()],
        out_specs=pl.BlockSpec((ts, d), lambda b, t: (b * n_t + t, 0)),
        out_shape=jax.ShapeDtypeStruct((m, d), BF16),
        scratch_shapes=[pltpu.VMEM((d // LANES, ts + CONV_HALO, LANES), F32),
                        pltpu.VMEM((d // LANES, ts, LANES), F32),
                        pltpu.VMEM((n_taps, d // LANES, SUBLANES, LANES), F32)],
        compiler_params=_params(2), name="conf_conv_prompt")(u, u, dw, dw_b, ln_g, ln_b)


SC_HALO = 8


def _short_conv_kernel(ts, n_taps, halo_ref, main_ref, gate_ref, w_ref, o_ref, win_ref):
    t = pl.program_id(1)
    d = main_ref.shape[1]
    first = SC_HALO - (n_taps - 1)
    for c in range(d // LANES):
        cols = slice(c * LANES, (c + 1) * LANES)
        win_ref[c, 0:SC_HALO, :] = jnp.where(t == 0, 0.0, halo_ref[:, cols])
        win_ref[c, SC_HALO:, :] = main_ref[:, cols]
    for c in range(d // LANES):
        cols = slice(c * LANES, (c + 1) * LANES)
        acc = w_ref[0:1, cols] * win_ref[c, first:first + ts, :]
        for k in range(1, n_taps):
            acc = acc + w_ref[k:k + 1, cols] * win_ref[c, first + k:first + k + ts, :]
        o_ref[:, cols] = (gate_ref[:, cols] * acc).astype(o_ref.dtype)


def short_conv_prompt(u, gate, w_conv, *, batch, seq, ts):
    m, d = u.shape
    n_taps = w_conv.shape[0]
    n_t = seq // ts
    halo_per_tile = ts // SC_HALO
    return pl.pallas_call(
        functools.partial(_short_conv_kernel, ts, n_taps),
        grid=(batch, n_t),
        in_specs=[
            pl.BlockSpec((SC_HALO, d),
                         lambda b, t: (jnp.maximum((b * n_t + t) * halo_per_tile - 1, 0), 0)),
            pl.BlockSpec((ts, d), lambda b, t: (b * n_t + t, 0)),
            pl.BlockSpec((ts, d), lambda b, t: (b * n_t + t, 0)),
            pl.BlockSpec((n_taps, d), lambda b, t: (0, 0))],
        out_specs=pl.BlockSpec((ts, d), lambda b, t: (b * n_t + t, 0)),
        out_shape=jax.ShapeDtypeStruct((m, d), BF16),
        scratch_shapes=[pltpu.VMEM((d // LANES, ts + SC_HALO, LANES), F32)],
        compiler_params=_params(2), name="short_conv_prompt")(u, u, gate, w_conv)


def _conf_conv_sample_kernel(n_hist, buf_ref, st_ref, u_ref, w_ref, b_ref, g_ref, beta_ref,
                             o_ref, ns_ref):
    del buf_ref
    u = u_ref[...]
    acc = w_ref[n_hist:n_hist + 1, :] * u + b_ref[...]
    for k in range(n_hist):
        acc = acc + w_ref[k:k + 1, :] * st_ref[k]
    o_ref[...] = _layer_norm_silu(acc, g_ref[...], beta_ref[...]).astype(o_ref.dtype)
    ns_ref[0:n_hist - 1] = st_ref[1:n_hist]
    ns_ref[n_hist - 1] = u


def conf_conv_sample(buf, state_t, u, dw, dw_b, ln_g, ln_b, layer, *, first_row, tb):
    n_hist, n_b, d = state_t.shape
    blk0 = first_row // tb
    vec = lambda: pl.BlockSpec((None, 1, d), lambda i: (layer, 0, 0))
    return pl.pallas_call(
        functools.partial(_conf_conv_sample_kernel, n_hist),
        grid=(n_b // tb,),
        in_specs=[pl.BlockSpec(memory_space=pl.ANY),
                  pl.BlockSpec((n_hist, tb, d), lambda i: (0, i, 0)),
                  pl.BlockSpec((tb, d), lambda i: (blk0 + i, 0)),
                  pl.BlockSpec((None, n_hist + 1, d), lambda i: (layer, 0, 0)),
                  vec(), vec(), vec()],
        out_specs=[pl.BlockSpec((tb, d), lambda i: (blk0 + i, 0)),
                   pl.BlockSpec((n_hist, tb, d), lambda i: (0, i, 0))],
        out_shape=[jax.ShapeDtypeStruct(buf.shape, buf.dtype),
                   jax.ShapeDtypeStruct(state_t.shape, state_t.dtype)],
        input_output_aliases={0: 0},
        compiler_params=_params(1), name="conf_conv_sample")(buf, state_t, u, dw, dw_b, ln_g, ln_b)


def _short_conv_sample_kernel(n_hist, buf_ref, st_ref, u_ref, gate_ref, w_ref, o_ref):
    del buf_ref
    d = u_ref.shape[1]
    acc = w_ref[n_hist:n_hist + 1, :] * u_ref[...]
    for k in range(n_hist):
        acc = acc + w_ref[k:k + 1, :] * st_ref[:, k * d:(k + 1) * d]
    o_ref[...] = (gate_ref[...] * acc).astype(o_ref.dtype)


def short_conv_sample(buf, state, u, gate, w_conv, *, first_row, tb):
    n_b, n_hist, d = state.shape
    st2 = state.reshape(n_b, n_hist * d)
    blk0 = first_row // tb
    return pl.pallas_call(
        functools.partial(_short_conv_sample_kernel, n_hist),
        grid=(n_b // tb,),
        in_specs=[pl.BlockSpec(memory_space=pl.ANY),
                  pl.BlockSpec((tb, n_hist * d), lambda i: (i, 0)),
                  pl.BlockSpec((tb, d), lambda i: (blk0 + i, 0)),
                  pl.BlockSpec((tb, d), lambda i: (blk0 + i, 0)),
                  pl.BlockSpec((n_hist + 1, d), lambda i: (0, 0))],
        out_specs=pl.BlockSpec((tb, d), lambda i: (blk0 + i, 0)),
        out_shape=jax.ShapeDtypeStruct(buf.shape, buf.dtype),
        input_output_aliases={0: 0},
        compiler_params=_params(1), name="short_conv_sample")(buf, st2, u, gate, w_conv)


ATT_TQ = 512
ATT_TK = 512


def _flash_kernel(seq, qn_ref, qp_ref, kn_ref, kp_ref, v_ref, o_ref):
    k_full = jnp.concatenate([kn_ref[...], kp_ref[...]], axis=-1)
    n_q = seq // ATT_TQ
    qs = [jnp.concatenate([qn_ref[i * ATT_TQ:(i + 1) * ATT_TQ, :],
                           qp_ref[i * ATT_TQ:(i + 1) * ATT_TQ, :]], axis=-1) for i in range(n_q)]
    m_run = [jnp.full((1, ATT_TQ), -jnp.inf, F32) for _ in range(n_q)]
    l_run = [jnp.zeros((1, ATT_TQ), F32) for _ in range(n_q)]
    acc = [jnp.zeros((o_ref.shape[1], ATT_TQ), F32) for _ in range(n_q)]
    pairs = [(ki, qi) for ki in range(seq // ATT_TK) for qi in range(n_q)
             if (qi + 1) * ATT_TQ > ki * ATT_TK]

    def scores_t(ki, qi):
        k0 = ki * ATT_TK
        st = lax.dot_general(k_full[k0:k0 + ATT_TK, :], qs[qi], _NT, preferred_element_type=F32)
        if k0 + ATT_TK > qi * ATT_TQ:
            key = k0 + lax.broadcasted_iota(jnp.int32, st.shape, 0)
            qpos = qi * ATT_TQ + lax.broadcasted_iota(jnp.int32, st.shape, 1)
            st = jnp.where(key <= qpos, st, -jnp.inf)
        return st

    st_next = scores_t(*pairs[0])
    for t, (ki, qi) in enumerate(pairs):
        st = st_next
        if t + 1 < len(pairs):
            st_next = scores_t(*pairs[t + 1])
        m_new = jnp.maximum(m_run[qi], jnp.max(st, axis=0, keepdims=True))
        corr = jnp.exp2(m_run[qi] - m_new)
        p = jnp.exp2(st - m_new)
        l_run[qi] = l_run[qi] * corr + jnp.sum(p, axis=0, keepdims=True)
        m_run[qi] = m_new
        acc[qi] = acc[qi] * corr + lax.dot_general(
            v_ref[ki * ATT_TK:(ki + 1) * ATT_TK, :], p.astype(BF16), _TN, preferred_element_type=F32)
    for qi in range(n_q):
        o_ref[qi * ATT_TQ:(qi + 1) * ATT_TQ, :] = (acc[qi] / l_run[qi]).T.astype(o_ref.dtype)


def flash_prompt(q_nope, q_pe, kv_up, kpe, *, batch, seq):
    hd = V_HEAD_DIM
    spec = lambda col: pl.BlockSpec((seq, hd), col)
    return pl.pallas_call(
        functools.partial(_flash_kernel, seq),
        grid=(batch, N_HEADS),
        in_specs=[spec(lambda b, h: (b, h)), spec(lambda b, h: (b, h)),
                  spec(lambda b, h: (b, h)), spec(lambda b, h: (b, 0)),
                  spec(lambda b, h: (b, h + N_HEADS))],
        out_specs=spec(lambda b, h: (b, h)),
        out_shape=jax.ShapeDtypeStruct((q_nope.shape[0], N_HEADS * hd), BF16),
        compiler_params=_params(2), name="flash_prompt")(q_nope, q_pe, kv_up, kpe, kv_up)


PAGES_PER_STEP = 32
PAGE_RING = 3


def _per_head_nt_kernel(x_ref, w_ref, o_ref):
    o_ref[0] = lax.dot_general(x_ref[...], w_ref[...].astype(BF16), _NT,
                               preferred_element_type=F32).astype(o_ref.dtype)


def absorb_q(q_nope, w_uk2, *, row_block, n_rows):
    r = w_uk2.shape[0]
    return pl.pallas_call(
        _per_head_nt_kernel, grid=(N_HEADS,),
        in_specs=[pl.BlockSpec((n_rows, QK_NOPE_DIM), lambda h: (row_block, h)),
                  pl.BlockSpec((r, QK_NOPE_DIM), lambda h: (0, h))],
        out_specs=pl.BlockSpec((1, n_rows, r), lambda h: (h, 0, 0)),
        out_shape=jax.ShapeDtypeStruct((N_HEADS, n_rows, r), BF16),
        compiler_params=_params(1), name="absorb_q")(q_nope, w_uk2)


def _per_head_nn_kernel(buf_ref, x_ref, w_ref, o_ref):
    del buf_ref
    o_ref[...] = jnp.dot(x_ref[0], w_ref[...].astype(BF16),
                         preferred_element_type=F32).astype(o_ref.dtype)


def expand_o(buf, o_lat_t, w_uv2, *, row_block):
    n_h, n_b, r = o_lat_t.shape
    return pl.pallas_call(
        _per_head_nn_kernel, grid=(n_h,),
        in_specs=[pl.BlockSpec(memory_space=pl.ANY),
                  pl.BlockSpec((1, n_b, r), lambda h: (h, 0, 0)),
                  pl.BlockSpec((r, V_HEAD_DIM), lambda h: (0, h))],
        out_specs=pl.BlockSpec((n_b, V_HEAD_DIM), lambda h: (row_block, h)),
        out_shape=jax.ShapeDtypeStruct(buf.shape, buf.dtype),
        input_output_aliases={0: 0},
        compiler_params=_params(1), name="expand_o")(buf, o_lat_t, w_uv2)


def _page_copies(pt_ref, ckv_hbm, kpt_hbm, ck_ring, kp_ring, sem_ck, sem_kp, chunk, n_groups):
    slot = chunk % PAGE_RING
    b = chunk // n_groups
    g = chunk % n_groups
    copies = []
    for i in range(PAGES_PER_STEP):
        pg = pt_ref[b, g * PAGES_PER_STEP + i]
        copies.append(pltpu.make_async_copy(ckv_hbm.at[pg], ck_ring.at[slot, i], sem_ck.at[slot]))
        copies.append(pltpu.make_async_copy(kpt_hbm.at[pg], kp_ring.at[slot, i], sem_kp.at[slot]))
    return copies


def _decode_kernel(pt_ref, ql_ref, qp_ref, cn_ref, kn_ref, ckv_hbm, kpt_hbm, o_ref,
                   ck_ring, kp_ring, sem_ck, sem_kp, ck_all, s_all):
    g = pl.program_id(1)
    n_groups = pl.num_programs(1)
    n_chunks = pl.num_programs(0) * n_groups
    chunk = pl.program_id(0) * n_groups + g
    page = ck_ring.shape[2]
    ring_args = (pt_ref, ckv_hbm, kpt_hbm, ck_ring, kp_ring, sem_ck, sem_kp)

    @pl.when(chunk == 0)
    def _():
        for c in range(PAGE_RING - 1):
            for cp in _page_copies(*ring_args, jnp.int32(c), n_groups):
                cp.start()

    @pl.when(chunk + (PAGE_RING - 1) < n_chunks)
    def _():
        for cp in _page_copies(*ring_args, chunk + (PAGE_RING - 1), n_groups):
            cp.start()

    for cp in _page_copies(*ring_args, chunk, n_groups):
        cp.wait()

    slot = chunk % PAGE_RING
    q = ql_ref[0]
    qp = qp_ref[0]
    for i in range(PAGES_PER_STEP):
        ck_page = ck_ring[slot, i].astype(BF16)
        ck_all[g, i * page:(i + 1) * page, :] = ck_page
        s_all[g, :, i * page:(i + 1) * page] = (
            lax.dot_general(q, ck_page, _NT, preferred_element_type=F32)
            + jnp.dot(qp, kp_ring[slot, i].astype(BF16), preferred_element_type=F32))

    @pl.when(g == n_groups - 1)
    def _():
        n_g = s_all.shape[0]
        c_new = cn_ref[0].astype(BF16).astype(F32)
        k_new = kn_ref[0].astype(BF16).astype(F32)
        s_self = (jnp.sum(q.astype(F32) * c_new, axis=-1, keepdims=True)
                  + jnp.sum(qp.astype(F32) * k_new, axis=-1, keepdims=True))
        m_fin = s_self
        for j in range(n_g):
            m_fin = jnp.maximum(m_fin, jnp.max(s_all[j], axis=-1, keepdims=True))
        p_self = jnp.exp2(s_self - m_fin)
        l_fin = p_self
        acc = p_self.astype(BF16).astype(F32) * c_new
        for j in range(n_g):
            p = jnp.exp2(s_all[j] - m_fin)
            l_fin = l_fin + jnp.sum(p, axis=-1, keepdims=True)
            acc = acc + jnp.dot(p.astype(BF16), ck_all[j], preferred_element_type=F32)
        o_ref[0] = (acc / l_fin).astype(o_ref.dtype)


def decode_attention(page_table, q_lat, q_pe, c_new, kpe_new, cache_ckv, cache_kpe_t):
    n_b, n_h, r = q_lat.shape
    p_dim = q_pe.shape[2]
    page = cache_ckv.shape[1]
    n_pages = page_table.shape[1]
    g_pages = PAGES_PER_STEP
    n_groups = n_pages // g_pages
    assert n_groups * g_pages == n_pages and n_b * n_groups >= PAGE_RING - 1

    in_specs = [pl.BlockSpec((1, n_h, r), lambda b, g, pt: (b, 0, 0)),
                pl.BlockSpec((1, n_h, p_dim), lambda b, g, pt: (b, 0, 0)),
                pl.BlockSpec((1, 1, r), lambda b, g, pt: (b, 0, 0)),
                pl.BlockSpec((1, 1, p_dim), lambda b, g, pt: (b, 0, 0)),
                pl.BlockSpec(memory_space=pl.ANY),
                pl.BlockSpec(memory_space=pl.ANY)]
    grid_spec = pltpu.PrefetchScalarGridSpec(
        num_scalar_prefetch=1, grid=(n_b, n_groups), in_specs=in_specs,
        out_specs=pl.BlockSpec((1, n_h, r), lambda b, g, pt: (b, 0, 0)),
        scratch_shapes=[pltpu.VMEM((PAGE_RING, g_pages, page, r), F32),
                        pltpu.VMEM((PAGE_RING, g_pages, p_dim, page), F32),
                        pltpu.SemaphoreType.DMA((PAGE_RING,)),
                        pltpu.SemaphoreType.DMA((PAGE_RING,)),
                        pltpu.VMEM((n_groups, g_pages * page, r), BF16),
                        pltpu.VMEM((n_groups, n_h, g_pages * page), F32)])
    return pl.pallas_call(
        _decode_kernel, grid_spec=grid_spec,
        out_shape=jax.ShapeDtypeStruct((n_b, n_h, r), BF16),
        compiler_params=_params(2), name="decode_attention")(
            page_table, q_lat, q_pe, c_new, kpe_new, cache_ckv, cache_kpe_t)


def _comb_identity(accs, ex):
    return (accs[0],)


def _comb_glu(accs, ex):
    return ((accs[0] + ex[0]) * _sigmoid(accs[1] + ex[1]),)


def _comb_short_conv_in(accs, ex):
    return (accs[0], accs[1] * accs[2])


def _comb_rms(accs, ex):
    y = _rms(accs[0], ex[0])
    return (y, y)


def _tile_lanes(tab, width):
    reps = width // tab.shape[1]
    return tab if reps == 1 else jnp.concatenate([tab] * reps, axis=-1)


def _comb_rope(scale, accs, ex):
    cos_t = _tile_lanes(ex[0], accs[0].shape[1])
    sin_t = _tile_lanes(ex[1], accs[0].shape[1])
    return ((accs[0] * cos_t + accs[1] * sin_t) * scale,)


def _comb_rope_f32_bf16(accs, ex):
    y = accs[0] * ex[0] + accs[1] * ex[1]
    return (y, y)


def _comb_scale(scale, accs, ex):
    return (accs[0] * scale,)


def _rot_cols(w):
    half = w.shape[-1] // 2
    return jnp.concatenate([-w[..., half:], w[..., :half]], axis=-1)


def _last_rows(u, n_batch, seq, n_keep):
    return jnp.stack([u[(b + 1) * seq - n_keep:(b + 1) * seq] for b in range(n_batch)])


def _pad_lanes(w):
    pad = LANES - w.shape[-1]
    return jnp.concatenate([w, jnp.zeros(w.shape[:-1] + (pad,), w.dtype)], axis=-1)


def kernel(x_prompt, x_sample, state_conv_l0, cache_ckv_l1, cache_kpe_l1, state_sconv_l2, state_conv_l3, page_table, norm_g, ca_w_in, ca_b_in, ca_dw, ca_dw_b, ca_ln_g, ca_ln_b, ca_w_out, ca_b_out, mla_w_dq, mla_q_norm, mla_w_uq, mla_w_dkv, mla_kv_norm, mla_w_uk, mla_w_uv, mla_w_o, sc_w_in, sc_w_conv, sc_w_out, ffn_w_gu, ffn_w_down):
    b_p, s_p, d = x_prompt.shape
    b_s = x_sample.shape[0]
    m_p = b_p * s_p
    depth = norm_g.shape[0]
    past_len = page_table.shape[1] * cache_ckv_l1.shape[1]
    conv_state_in = (state_conv_l0, state_conv_l3)
    tm = ROW_TILE
    sample_block = m_p // b_s

    gains = norm_g.reshape(depth * 4, 1, d)
    ca_b_in3 = ca_b_in[:, None, :]
    ca_b_out3 = ca_b_out[:, None, :]
    ca_dw_b3 = ca_dw_b[:, None, :]
    ca_ln_g3 = ca_ln_g[:, None, :]
    ca_ln_b3 = ca_ln_b[:, None, :]

    x, h = stack_and_norm(x_prompt.reshape(m_p, d), x_sample.reshape(b_s, d), gains, 0, tm=tm)

    half = QK_ROPE_DIM // 2
    inv = ROPE_THETA ** (-jnp.arange(half, dtype=F32) / half)
    pos = jnp.concatenate([jnp.arange(s_p), jnp.full((1,), past_len, jnp.int32)])
    ang = pos.astype(F32)[:, None] * inv[None, :]

    def rope_table(t):
        t = _pad_lanes(jnp.concatenate([t, t], axis=-1))
        return jnp.concatenate([jnp.tile(t[:s_p], (b_p, 1)), jnp.tile(t[s_p:], (b_s, 1))], axis=0)

    cos_tab = rope_table(jnp.cos(ang))
    sin_tab = rope_table(jnp.sin(ang))

    outs_conv = []
    for i in range(depth):
        kind = i % N_MIXERS
        g_post, g_ffn = 4 * i + 1, 4 * i + 2
        if kind == 0:
            j = i // N_MIXERS
            nb = d // 512
            (u,) = multi_mm("conf_in", h, [(ca_w_in, j, 0), (ca_w_in, j, nb)],
                            [(ca_b_in3, "col", j, 0), (ca_b_in3, "col", j, nb)],
                            _comb_glu, [F32], n_cols=d, tm=tm, tn=512)
            v = conf_conv_prompt(u, ca_dw, ca_dw_b3, ca_ln_g3, ca_ln_b3, j, batch=b_p, seq=s_p, ts=256)
            state_t = jnp.transpose(conv_state_in[j], (1, 0, 2))
            v, new_state_t = conf_conv_sample(v, state_t, u, ca_dw, ca_dw_b3, ca_ln_g3, ca_ln_b3, j,
                                              first_row=m_p, tb=16)
            x, h = out_proj("conf_out", v, ca_w_out, j, ca_b_out3, x, gains, g_post, g_ffn,
                            tm=OUT_PROJ_TILE)
            conv_p = _last_rows(u, b_p, s_p, ca_dw.shape[1] - 1)
            conv_s = jnp.transpose(new_state_t, (1, 0, 2))
            outs_conv.append((conv_p, conv_s))
        elif kind == 1:
            r = KV_LORA_RANK
            qr = mla_w_dq.shape[1]
            w_down = jnp.concatenate([mla_w_dq, mla_w_dkv[:, :r]], axis=1)[None]
            g_down = jnp.concatenate([mla_q_norm, mla_kv_norm])[None, None]
            lat_f32, lat_bf = multi_mm("mla_down", h, [(w_down, 0, 0)], [(g_down, "col", 0, 0)],
                                       _comb_rms, [F32, BF16], n_cols=qr + r, tm=tm, tn=512)
            w_kpe = mla_w_dkv[:, r:]
            kpe, kpe_bf = multi_mm("mla_kpe", h,
                                   [(_pad_lanes(w_kpe)[None], 0, 0), (_pad_lanes(_rot_cols(w_kpe))[None], 0, 0)],
                                   [(cos_tab, "row", None, None), (sin_tab, "row", None, None)],
                                   _comb_rope_f32_bf16, [F32, BF16], n_cols=LANES, tm=tm, tn=LANES)
            w_uq3 = mla_w_uq.reshape(qr, N_HEADS, QK_NOPE_DIM + QK_ROPE_DIM)
            w_qn = w_uq3[:, :, :QK_NOPE_DIM].reshape(1, qr, N_HEADS * QK_NOPE_DIM)
            w_qp = w_uq3[:, :, QK_NOPE_DIM:]
            w_qp_a = _pad_lanes(w_qp).reshape(1, qr, N_HEADS * LANES)
            w_qp_b = _pad_lanes(_rot_cols(w_qp)).reshape(1, qr, N_HEADS * LANES)
            (q_nope,) = multi_mm("mla_qn", lat_bf, [(w_qn, 0, 0)], [],
                                 functools.partial(_comb_scale, Q_SCALE), [BF16],
                                 n_cols=N_HEADS * QK_NOPE_DIM, tm=tm, tn=1024, x_cols=(0, qr))
            (q_pe,) = multi_mm("mla_qp", lat_bf, [(w_qp_a, 0, 0), (w_qp_b, 0, 0)],
                               [(cos_tab, "row", None, None), (sin_tab, "row", None, None)],
                               functools.partial(_comb_rope, Q_SCALE), [BF16],
                               n_cols=N_HEADS * LANES, tm=tm, tn=1024, x_cols=(0, qr))
            w_uk2 = mla_w_uk.reshape(r, N_HEADS * QK_NOPE_DIM)
            w_uv2 = mla_w_uv.reshape(r, N_HEADS * V_HEAD_DIM)
            w_up = jnp.concatenate([w_uk2, w_uv2], axis=1)[None]
            (kv_up,) = multi_mm("mla_kv_up", lat_bf, [(w_up, 0, 0)], [], _comb_identity, [BF16],
                                n_cols=w_up.shape[2], tm=tm, tn=1024, x_cols=(qr // r, r))
            att = flash_prompt(q_nope, q_pe, kv_up, kpe_bf, batch=b_p, seq=s_p)
            q_lat = absorb_q(q_nope, w_uk2, row_block=sample_block, n_rows=b_s).transpose(1, 0, 2)
            q_pe_s = q_pe[m_p:].reshape(b_s, N_HEADS, LANES)[:, :, :QK_ROPE_DIM]
            ckv_s = lat_f32[m_p:, qr:].reshape(b_s, 1, r)
            kpe_s = kpe[m_p:, :QK_ROPE_DIM].reshape(b_s, 1, QK_ROPE_DIM)
            cache_kpe_t = jnp.swapaxes(cache_kpe_l1, 1, 2)
            o_lat = decode_attention(page_table, q_lat, q_pe_s, ckv_s, kpe_s, cache_ckv_l1, cache_kpe_t)
            att = expand_o(att, o_lat.transpose(1, 0, 2), w_uv2, row_block=sample_block)
            x, h = out_proj("mla_out", att, mla_w_o[None], 0, None, x, gains, g_post, g_ffn,
                            tm=OUT_PROJ_TILE)
            ckv_p = lat_f32[:m_p, qr:].reshape(b_p, s_p, r)
            kpe_p = kpe[:m_p, :QK_ROPE_DIM].reshape(b_p, s_p, QK_ROPE_DIM)
        else:
            nb = d // 512
            sc_w_in3 = sc_w_in[None]
            gate, u = multi_mm("sc_in", h, [(sc_w_in3, 0, 0), (sc_w_in3, 0, nb), (sc_w_in3, 0, 2 * nb)], [],
                               _comb_short_conv_in, [F32, F32], n_cols=d, tm=tm, tn=512)
            v = short_conv_prompt(u, gate, sc_w_conv, batch=b_p, seq=s_p, ts=256)
            v = short_conv_sample(v, state_sconv_l2, u, gate, sc_w_conv, first_row=m_p, tb=32)
            x, h = out_proj("sc_out", v, sc_w_out[None], 0, None, x, gains, g_post, g_ffn,
                            tm=OUT_PROJ_TILE)
            n_keep = sc_w_conv.shape[0] - 1
            u_s = u[m_p:]
            sc_p = _last_rows(u, b_p, s_p, n_keep)
            sc_s = jnp.concatenate([state_sconv_l2[:, 1:], u_s[:, None, :]], axis=1)

        g_next = 4 * (i + 1) if i + 1 < depth else None
        x, h = ffn(h, x, ffn_w_gu, ffn_w_down, i, gains, 4 * i + 3, g_next,
                   tm=tm, tf=FFN_CHUNK, n_tail=b_s)

    y_prompt = x.reshape(b_p, s_p, d)
    y_sample = h.reshape(b_s, 1, d)
    return (y_prompt, y_sample, outs_conv[0][0], outs_conv[0][1], ckv_p, kpe_p,
            ckv_s, kpe_s, sc_p, sc_s, outs_conv[1][0], outs_conv[1][1])
```

```python
import functools

import jax
import jax.numpy as jnp
from jax import lax
from jax.experimental import pallas as pl
from jax.experimental.pallas import tpu as pltpu

F32 = jnp.float32
BF16 = jnp.bfloat16

N_HEADS = 16
QK_NOPE_DIM = 128
QK_ROPE_DIM = 64
V_HEAD_DIM = 128
KV_LORA_RANK = 512
ROPE_THETA = 10000.0
ATTN_SCALE = (QK_NOPE_DIM + QK_ROPE_DIM) ** -0.5
LOG2_E = 1.4426950408889634
Q_SCALE = ATTN_SCALE * LOG2_E
NORM_EPS = 1e-6
N_MIXERS = 3
LANES = 128

VMEM_LIMIT_BYTES = 58 * 1024 * 1024

ROW_TILE = 1040
OUT_PROJ_TILE = 416
OUT_PROJ_SPLIT = 2
FFN_CHUNK = 256

_SINGLE = pl.Buffered(1)
_NT = (((1,), (1,)), ((), ()))
_TN = (((0,), (0,)), ((), ()))


def _params(n_grid_axes):
    return pltpu.CompilerParams(
        dimension_semantics=("arbitrary",) * n_grid_axes,
        vmem_limit_bytes=VMEM_LIMIT_BYTES)


def _sigmoid(x):
    return 1.0 / (1.0 + jnp.exp(-x))


def _rms(x, g):
    return x * lax.rsqrt(jnp.mean(x * x, axis=-1, keepdims=True) + NORM_EPS) * g


def _layer_norm_silu(v, g, beta):
    mu = jnp.mean(v, axis=-1, keepdims=True)
    var = jnp.mean(jnp.square(v - mu), axis=-1, keepdims=True)
    y = (v - mu) * lax.rsqrt(var + NORM_EPS) * g + beta
    return y * _sigmoid(y)


def _stack_kernel(n_tail, xp_ref, xs_ref, g_ref, x_ref, h_ref):
    x_ref[...] = xp_ref[...]

    @pl.when(pl.program_id(0) == pl.num_programs(0) - 1)
    def _():
        x_ref[x_ref.shape[0] - n_tail:, :] = xs_ref[...]

    x = x_ref[...]
    h_ref[...] = _rms(x, g_ref[...]).astype(h_ref.dtype)


def stack_and_norm(xp, xs, gains, g_idx, *, tm):
    m_p, d = xp.shape
    m_s = xs.shape[0]
    m = m_p + m_s
    assert m % tm == 0 and m - m_p == m_s and tm - m_s >= 0 and (m_p % tm) == tm - m_s
    return pl.pallas_call(
        functools.partial(_stack_kernel, m_s),
        grid=(m // tm,),
        in_specs=[pl.BlockSpec((tm, d), lambda i: (i, 0)),
                  pl.BlockSpec((m_s, d), lambda i: (0, 0)),
                  pl.BlockSpec((None, 1, d), lambda i: (g_idx, 0, 0))],
        out_specs=[pl.BlockSpec((tm, d), lambda i: (i, 0)), pl.BlockSpec((tm, d), lambda i: (i, 0))],
        out_shape=[jax.ShapeDtypeStruct((m, d), F32), jax.ShapeDtypeStruct((m, d), BF16)],
        compiler_params=_params(1), name="stack_and_norm")(xp, xs, gains)


def _mm_kernel(n_w, n_ex, combine, x_ref, *refs):
    w_refs = refs[:n_w]
    ex_refs = refs[n_w:n_w + n_ex]
    out_refs = refs[n_w + n_ex:len(refs) - n_w]
    wb_refs = refs[len(refs) - n_w:]

    @pl.when(pl.program_id(1) == 0)
    def _():
        for w, wb in zip(w_refs, wb_refs):
            wb[...] = w[...].astype(BF16)

    x = x_ref[...]
    accs = [jnp.dot(x, wb[...], preferred_element_type=F32) for wb in wb_refs]
    outs = combine(accs, [e[...] for e in ex_refs])
    for o_ref, o in zip(out_refs, outs):
        o_ref[...] = o.astype(o_ref.dtype)


def multi_mm(name, x, ws, extras, combine, out_dtypes, *, n_cols, tm, tn, x_cols=None):
    m = x.shape[0]
    x_blk, k = (0, x.shape[1]) if x_cols is None else x_cols
    in_specs = [pl.BlockSpec((tm, k), lambda j, i: (i, x_blk))]
    args = [x]
    for w, layer, off in ws:
        in_specs.append(pl.BlockSpec((None, k, tn), lambda j, i, layer=layer, off=off: (layer, 0, j + off)))
        args.append(w)
    for arr, kind, layer, off in extras:
        if kind == "row":
            in_specs.append(pl.BlockSpec((tm, arr.shape[1]), lambda j, i: (i, 0)))
        else:
            in_specs.append(pl.BlockSpec((None, 1, tn), lambda j, i, layer=layer, off=off: (layer, 0, j + off)))
        args.append(arr)
    out_shape = [jax.ShapeDtypeStruct((m, n_cols), dt) for dt in out_dtypes]
    out_specs = [pl.BlockSpec((tm, tn), lambda j, i: (i, j)) for _ in out_dtypes]
    return pl.pallas_call(
        functools.partial(_mm_kernel, len(ws), len(extras), combine),
        grid=(n_cols // tn, m // tm), in_specs=in_specs, out_specs=out_specs, out_shape=out_shape,
        scratch_shapes=[pltpu.VMEM((k, tn), BF16) for _ in ws],
        compiler_params=_params(2), name=name)(*args)


def _out_proj_kernel(has_bias, v_ref, w_ref, *refs):
    if has_bias:
        b_ref, refs = refs[0], refs[1:]
    x_ref, g1_ref, g2_ref, xo_ref, h_ref, wb_ref = refs

    @pl.when(pl.program_id(0) == 0)
    def _():
        wb_ref[...] = w_ref[...].astype(BF16)

    rows = v_ref.shape[0] // OUT_PROJ_SPLIT
    for s in range(OUT_PROJ_SPLIT):
        sl = slice(s * rows, (s + 1) * rows)
        y = jnp.dot(v_ref[sl, :], wb_ref[...], preferred_element_type=F32)
        if has_bias:
            y = y + b_ref[...]
        x_new = x_ref[sl, :] + _rms(y, g1_ref[...])
        xo_ref[sl, :] = x_new
        h_ref[sl, :] = _rms(x_new, g2_ref[...]).astype(h_ref.dtype)


def out_proj(name, v, w, layer, bias, x, gains, g_post, g_next, *, tm):
    m, k = v.shape
    d = w.shape[2]
    in_specs = [pl.BlockSpec((tm, k), lambda i: (i, 0)),
                pl.BlockSpec((None, k, d), lambda i: (layer, 0, 0), pipeline_mode=_SINGLE)]
    args = [v, w]
    if bias is not None:
        in_specs.append(pl.BlockSpec((None, 1, d), lambda i: (layer, 0, 0)))
        args.append(bias)
    in_specs += [pl.BlockSpec((tm, d), lambda i: (i, 0)),
                 pl.BlockSpec((None, 1, d), lambda i: (g_post, 0, 0)),
                 pl.BlockSpec((None, 1, d), lambda i: (g_next, 0, 0))]
    args += [x, gains, gains]
    return pl.pallas_call(
        functools.partial(_out_proj_kernel, bias is not None),
        grid=(m // tm,), in_specs=in_specs,
        out_specs=[pl.BlockSpec((tm, d), lambda i: (i, 0)), pl.BlockSpec((tm, d), lambda i: (i, 0))],
        out_shape=[jax.ShapeDtypeStruct((m, d), F32), jax.ShapeDtypeStruct((m, d), BF16)],
        scratch_shapes=[pltpu.VMEM((k, d), BF16)],
        compiler_params=_params(1), name=name)(*args)


def _ffn_partial(h_ref, wg_ref, wu_ref, wd_ref):
    h = h_ref[...]
    g = jnp.dot(h, wg_ref[...].astype(BF16), preferred_element_type=F32)
    u = jnp.dot(h, wu_ref[...].astype(BF16), preferred_element_type=F32)
    act = (g * _sigmoid(g) * u).astype(BF16)
    return jnp.dot(act, wd_ref[...].astype(BF16), preferred_element_type=F32)


def _ffn_kernel(h_ref, x_ref, wg_ref, wu_ref, wd_ref, g1_ref, g2_ref, xo_ref, hn_ref):
    f = pl.program_id(1)

    @pl.when(f == 0)
    def _():
        xo_ref[...] = jnp.zeros(xo_ref.shape, F32)

    xo_ref[...] += _ffn_partial(h_ref, wg_ref, wu_ref, wd_ref)

    @pl.when(f == pl.num_programs(1) - 1)
    def _():
        x_new = x_ref[...] + _rms(xo_ref[...], g1_ref[...])
        xo_ref[...] = x_new
        hn_ref[...] = _rms(x_new, g2_ref[...]).astype(hn_ref.dtype)


def _ffn_last_kernel(n_tail, h_ref, x_ref, wg_ref, wu_ref, wd_ref, g1_ref, yp_ref, ys_ref):
    f = pl.program_id(1)

    @pl.when(f == 0)
    def _():
        yp_ref[...] = jnp.zeros(yp_ref.shape, F32)

    yp_ref[...] += _ffn_partial(h_ref, wg_ref, wu_ref, wd_ref)

    @pl.when(f == pl.num_programs(1) - 1)
    def _():
        yp_ref[...] = x_ref[...] + _rms(yp_ref[...], g1_ref[...])

        @pl.when(pl.program_id(0) == pl.num_programs(0) - 1)
        def _():
            ys_ref[...] = yp_ref[yp_ref.shape[0] - n_tail:, :]


def ffn(h, x, w_gu, w_down, layer, gains, g_post, g_next, *, tm, tf, n_tail):
    m, d = h.shape
    hidden = w_down.shape[1]
    n_f = hidden // tf
    in_specs = [pl.BlockSpec((tm, d), lambda i, f: (i, 0)),
                pl.BlockSpec((tm, d), lambda i, f: (i, 0)),
                pl.BlockSpec((None, d, tf), lambda i, f: (layer, 0, f)),
                pl.BlockSpec((None, d, tf), lambda i, f: (layer, 0, f + n_f)),
                pl.BlockSpec((None, tf, d), lambda i, f: (layer, f, 0)),
                pl.BlockSpec((None, 1, d), lambda i, f: (g_post, 0, 0))]
    args = [h, x, w_gu, w_gu, w_down, gains]
    row_spec = lambda: pl.BlockSpec((tm, d), lambda i, f: (i, 0), pipeline_mode=_SINGLE)
    if g_next is not None:
        in_specs.append(pl.BlockSpec((None, 1, d), lambda i, f: (g_next, 0, 0)))
        args.append(gains)
        return pl.pallas_call(
            _ffn_kernel, grid=(m // tm, n_f), in_specs=in_specs,
            out_specs=[row_spec(), row_spec()],
            out_shape=[jax.ShapeDtypeStruct((m, d), F32), jax.ShapeDtypeStruct((m, d), BF16)],
            compiler_params=_params(2), name="ffn")(*args)
    assert (m - n_tail) % tm == tm - n_tail
    return pl.pallas_call(
        functools.partial(_ffn_last_kernel, n_tail), grid=(m // tm, n_f), in_specs=in_specs,
        out_specs=[row_spec(), pl.BlockSpec((n_tail, d), lambda i, f: (0, 0))],
        out_shape=[jax.ShapeDtypeStruct((m - n_tail, d), F32), jax.ShapeDtypeStruct((n_tail, d), F32)],
        compiler_params=_params(2), name="ffn_last")(*args)


CONV_HALO = 32
CONV_ROWS = 128
SUBLANES = 8


def _conf_conv_kernel(ts, n_taps, halo_ref, main_ref, w_ref, b_ref, g_ref, beta_ref, o_ref,
                      win_ref, conv_ref, w8_ref):
    t = pl.program_id(1)
    d = main_ref.shape[1]
    n_c = d // LANES
    n_sub = CONV_ROWS // SUBLANES
    col_slices = [slice(c * LANES, (c + 1) * LANES) for c in range(n_c)]

    @pl.when((pl.program_id(0) == 0) & (t == 0))
    def _():
        for k in range(n_taps):
            for c, cols in enumerate(col_slices):
                w8_ref[k, c] = jnp.broadcast_to(w_ref[k:k + 1, cols], (SUBLANES, LANES))

    for c, cols in enumerate(col_slices):
        win_ref[c, 0:CONV_HALO, :] = jnp.where(t == 0, 0.0, halo_ref[:, cols])
        win_ref[c, CONV_HALO:, :] = main_ref[:, cols]
    first = CONV_HALO - (n_taps - 1)

    def column(c, carry):
        def row_chunk(rc, carry2):
            r0 = pl.multiple_of(rc * CONV_ROWS, CONV_ROWS)
            accs = [None] * n_sub
            for k in range(n_taps):
                src = win_ref[c, pl.ds(r0 + first + k, CONV_ROWS), :]
                wk = w8_ref[k, c]
                for rr in range(n_sub):
                    term = wk * src[rr * SUBLANES:(rr + 1) * SUBLANES, :]
                    accs[rr] = term if k == 0 else accs[rr] + term
            conv_ref[c, pl.ds(r0, CONV_ROWS), :] = jnp.concatenate(accs, axis=0)
            return carry2

        return lax.fori_loop(0, ts // CONV_ROWS, row_chunk, carry)

    lax.fori_loop(0, n_c, column, 0)

    vs = [conv_ref[c] + b_ref[:, cols] for c, cols in enumerate(col_slices)]
    mu = jnp.sum(functools.reduce(jnp.add, vs), axis=-1, keepdims=True) / d
    sq = functools.reduce(jnp.add, [jnp.square(v - mu) for v in vs])
    rstd = lax.rsqrt(jnp.sum(sq, axis=-1, keepdims=True) / d + NORM_EPS)
    for v, cols in zip(vs, col_slices):
        y = (v - mu) * rstd * g_ref[:, cols] + beta_ref[:, cols]
        o_ref[:, cols] = (y * _sigmoid(y)).astype(o_ref.dtype)


def conf_conv_prompt(u, dw, dw_b, ln_g, ln_b, layer, *, batch, seq, ts):
    m, d = u.shape
    n_taps = dw.shape[1]
    n_t = seq // ts
    halo_per_tile = ts // CONV_HALO
    vec = lambda: pl.BlockSpec((None, 1, d), lambda b, t: (layer, 0, 0))
    return pl.pallas_call(
        functools.partial(_conf_conv_kernel, ts, n_taps),
        grid=(batch, n_t),
        in_specs=[
            pl.BlockSpec((CONV_HALO, d),
                         lambda b, t: (jnp.maximum((b * n_t + t) * halo_per_tile - 1, 0), 0)),
            pl.BlockSpec((ts, d), lambda b, t: (b * n_t + t, 0)),
            pl.BlockSpec((None, n_taps, d), lambda b, t: (layer, 0, 0)),
            vec(), vec(), vec()],
        out_specs=pl.BlockSpec((ts, d), lambda b, t: (b * n_t + t, 0)),
        out_shape=jax.ShapeDtypeStruct((m, d), BF16),
        scratch_shapes=[pltpu.VMEM((d // LANES, ts + CONV_HALO, LANES), F32),
                        pltpu.VMEM((d // LANES, ts, LANES), F32),
                        pltpu.VMEM((n_taps, d // LANES, SUBLANES, LANES), F32)],
        compiler_params=_params(2), name="conf_conv_prompt")(u, u, dw, dw_b, ln_g, ln_b)


SC_HALO = 8


def _short_conv_kernel(ts, n_taps, halo_ref, main_ref, gate_ref, w_ref, o_ref, win_ref):
    t = pl.program_id(1)
    d = main_ref.shape[1]
    first = SC_HALO - (n_taps - 1)
    for c in range(d // LANES):
        cols = slice(c * LANES, (c + 1) * LANES)
        win_ref[c, 0:SC_HALO, :] = jnp.where(t == 0, 0.0, halo_ref[:, cols])
        win_ref[c, SC_HALO:, :] = main_ref[:, cols]
    for c in range(d // LANES):
        cols = slice(c * LANES, (c + 1) * LANES)
        acc = w_ref[0:1, cols] * win_ref[c, first:first + ts, :]
        for k in range(1, n_taps):
            acc = acc + w_ref[k:k + 1, cols] * win_ref[c, first + k:first + k + ts, :]
        o_ref[:, cols] = (gate_ref[:, cols] * acc).astype(o_ref.dtype)


def short_conv_prompt(u, gate, w_conv, *, batch, seq, ts):
    m, d = u.shape
    n_taps = w_conv.shape[0]
    n_t = seq // ts
    halo_per_tile = ts // SC_HALO
    return pl.pallas_call(
        functools.partial(_short_conv_kernel, ts, n_taps),
        grid=(batch, n_t),
        in_specs=[
            pl.BlockSpec((SC_HALO, d),
                         lambda b, t: (jnp.maximum((b * n_t + t) * halo_per_tile - 1, 0), 0)),
            pl.BlockSpec((ts, d), lambda b, t: (b * n_t + t, 0)),
            pl.BlockSpec((ts, d), lambda b, t: (b * n_t + t, 0)),
            pl.BlockSpec((n_taps, d), lambda b, t: (0, 0))],
        out_specs=pl.BlockSpec((ts, d), lambda b, t: (b * n_t + t, 0)),
        out_shape=jax.ShapeDtypeStruct((m, d), BF16),
        scratch_shapes=[pltpu.VMEM((d // LANES, ts + SC_HALO, LANES), F32)],
        compiler_params=_params(2), name="short_conv_prompt")(u, u, gate, w_conv)


def _conf_conv_sample_kernel(n_hist, buf_ref, st_ref, u_ref, w_ref, b_ref, g_ref, beta_ref,
                             o_ref, ns_ref):
    del buf_ref
    u = u_ref[...]
    acc = w_ref[n_hist:n_hist + 1, :] * u + b_ref[...]
    for k in range(n_hist):
        acc = acc + w_ref[k:k + 1, :] * st_ref[k]
    o_ref[...] = _layer_norm_silu(acc, g_ref[...], beta_ref[...]).astype(o_ref.dtype)
    ns_ref[0:n_hist - 1] = st_ref[1:n_hist]
    ns_ref[n_hist - 1] = u


def conf_conv_sample(buf, state_t, u, dw, dw_b, ln_g, ln_b, layer, *, first_row, tb):
    n_hist, n_b, d = state_t.shape
    blk0 = first_row // tb
    vec = lambda: pl.BlockSpec((None, 1, d), lambda i: (layer, 0, 0))
    return pl.pallas_call(
        functools.partial(_conf_conv_sample_kernel, n_hist),
        grid=(n_b // tb,),
        in_specs=[pl.BlockSpec(memory_space=pl.ANY),
                  pl.BlockSpec((n_hist, tb, d), lambda i: (0, i, 0)),
                  pl.BlockSpec((tb, d), lambda i: (blk0 + i, 0)),
                  pl.BlockSpec((None, n_hist + 1, d), lambda i: (layer, 0, 0)),
                  vec(), vec(), vec()],
        out_specs=[pl.BlockSpec((tb, d), lambda i: (blk0 + i, 0)),
                   pl.BlockSpec((n_hist, tb, d), lambda i: (0, i, 0))],
        out_shape=[jax.ShapeDtypeStruct(buf.shape, buf.dtype),
                   jax.ShapeDtypeStruct(state_t.shape, state_t.dtype)],
        input_output_aliases={0: 0},
        compiler_params=_params(1), name="conf_conv_sample")(buf, state_t, u, dw, dw_b, ln_g, ln_b)


def _short_conv_sample_kernel(n_hist, buf_ref, st_ref, u_ref, gate_ref, w_ref, o_ref):
    del buf_ref
    d = u_ref.shape[1]
    acc = w_ref[n_hist:n_hist + 1, :] * u_ref[...]
    for k in range(n_hist):
        acc = acc + w_ref[k:k + 1, :] * st_ref[:, k * d:(k + 1) * d]
    o_ref[...] = (gate_ref[...] * acc).astype(o_ref.dtype)


def short_conv_sample(buf, state, u, gate, w_conv, *, first_row, tb):
    n_b, n_hist, d = state.shape
    st2 = state.reshape(n_b, n_hist * d)
    blk0 = first_row // tb
    return pl.pallas_call(
        functools.partial(_short_conv_sample_kernel, n_hist),
        grid=(n_b // tb,),
        in_specs=[pl.BlockSpec(memory_space=pl.ANY),
                  pl.BlockSpec((tb, n_hist * d), lambda i: (i, 0)),
                  pl.BlockSpec((tb, d), lambda i: (blk0 + i, 0)),
                  pl.BlockSpec((tb, d), lambda i: (blk0 + i, 0)),
                  pl.BlockSpec((n_hist + 1, d), lambda i: (0, 0))],
        out_specs=pl.BlockSpec((tb, d), lambda i: (blk0 + i, 0)),
        out_shape=jax.ShapeDtypeStruct(buf.shape, buf.dtype),
        input_output_aliases={0: 0},
        compiler_params=_params(1), name="short_conv_sample")(buf, st2, u, gate, w_conv)


ATT_TQ = 512
ATT_TK = 512


def _flash_kernel(seq, qn_ref, qp_ref, kn_ref, kp_ref, v_ref, o_ref):
    k_full = jnp.concatenate([kn_ref[...], kp_ref[...]], axis=-1)
    n_q = seq // ATT_TQ
    qs = [jnp.concatenate([qn_ref[i * ATT_TQ:(i + 1) * ATT_TQ, :],
                           qp_ref[i * ATT_TQ:(i + 1) * ATT_TQ, :]], axis=-1) for i in range(n_q)]
    m_run = [jnp.full((1, ATT_TQ), -jnp.inf, F32) for _ in range(n_q)]
    l_run = [jnp.zeros((1, ATT_TQ), F32) for _ in range(n_q)]
    acc = [jnp.zeros((o_ref.shape[1], ATT_TQ), F32) for _ in range(n_q)]
    pairs = [(ki, qi) for ki in range(seq // ATT_TK) for qi in range(n_q)
             if (qi + 1) * ATT_TQ > ki * ATT_TK]

    def scores_t(ki, qi):
        k0 = ki * ATT_TK
        st = lax.dot_general(k_full[k0:k0 + ATT_TK, :], qs[qi], _NT, preferred_element_type=F32)
        if k0 + ATT_TK > qi * ATT_TQ:
            key = k0 + lax.broadcasted_iota(jnp.int32, st.shape, 0)
            qpos = qi * ATT_TQ + lax.broadcasted_iota(jnp.int32, st.shape, 1)
            st = jnp.where(key <= qpos, st, -jnp.inf)
        return st

    st_next = scores_t(*pairs[0])
    for t, (ki, qi) in enumerate(pairs):
        st = st_next
        if t + 1 < len(pairs):
            st_next = scores_t(*pairs[t + 1])
        m_new = jnp.maximum(m_run[qi], jnp.max(st, axis=0, keepdims=True))
        corr = jnp.exp2(m_run[qi] - m_new)
        p = jnp.exp2(st - m_new)
        l_run[qi] = l_run[qi] * corr + jnp.sum(p, axis=0, keepdims=True)
        m_run[qi] = m_new
        acc[qi] = acc[qi] * corr + lax.dot_general(
            v_ref[ki * ATT_TK:(ki + 1) * ATT_TK, :], p.astype(BF16), _TN, preferred_element_type=F32)
    for qi in range(n_q):
        o_ref[qi * ATT_TQ:(qi + 1) * ATT_TQ, :] = (acc[qi] / l_run[qi]).T.astype(o_ref.dtype)


def flash_prompt(q_nope, q_pe, kv_up, kpe, *, batch, seq):
    hd = V_HEAD_DIM
    spec = lambda col: pl.BlockSpec((seq, hd), col)
    return pl.pallas_call(
        functools.partial(_flash_kernel, seq),
        grid=(batch, N_HEADS),
        in_specs=[spec(lambda b, h: (b, h)), spec(lambda b, h: (b, h)),
                  spec(lambda b, h: (b, h)), spec(lambda b, h: (b, 0)),
                  spec(lambda b, h: (b, h + N_HEADS))],
        out_specs=spec(lambda b, h: (b, h)),
        out_shape=jax.ShapeDtypeStruct((q_nope.shape[0], N_HEADS * hd), BF16),
        compiler_params=_params(2), name="flash_prompt")(q_nope, q_pe, kv_up, kpe, kv_up)


PAGES_PER_STEP = 32
PAGE_RING = 3


def _per_head_nt_kernel(x_ref, w_ref, o_ref):
    o_ref[0] = lax.dot_general(x_ref[...], w_ref[...].astype(BF16), _NT,
                               preferred_element_type=F32).astype(o_ref.dtype)


def absorb_q(q_nope, w_uk2, *, row_block, n_rows):
    r = w_uk2.shape[0]
    return pl.pallas_call(
        _per_head_nt_kernel, grid=(N_HEADS,),
        in_specs=[pl.BlockSpec((n_rows, QK_NOPE_DIM), lambda h: (row_block, h)),
                  pl.BlockSpec((r, QK_NOPE_DIM), lambda h: (0, h))],
        out_specs=pl.BlockSpec((1, n_rows, r), lambda h: (h, 0, 0)),
        out_shape=jax.ShapeDtypeStruct((N_HEADS, n_rows, r), BF16),
        compiler_params=_params(1), name="absorb_q")(q_nope, w_uk2)


def _per_head_nn_kernel(buf_ref, x_ref, w_ref, o_ref):
    del buf_ref
    o_ref[...] = jnp.dot(x_ref[0], w_ref[...].astype(BF16),
                         preferred_element_type=F32).astype(o_ref.dtype)


def expand_o(buf, o_lat_t, w_uv2, *, row_block):
    n_h, n_b, r = o_lat_t.shape
    return pl.pallas_call(
        _per_head_nn_kernel, grid=(n_h,),
        in_specs=[pl.BlockSpec(memory_space=pl.ANY),
                  pl.BlockSpec((1, n_b, r), lambda h: (h, 0, 0)),
                  pl.BlockSpec((r, V_HEAD_DIM), lambda h: (0, h))],
        out_specs=pl.BlockSpec((n_b, V_HEAD_DIM), lambda h: (row_block, h)),
        out_shape=jax.ShapeDtypeStruct(buf.shape, buf.dtype),
        input_output_aliases={0: 0},
        compiler_params=_params(1), name="expand_o")(buf, o_lat_t, w_uv2)


def _page_copies(pt_ref, ckv_hbm, kpt_hbm, ck_ring, kp_ring, sem_ck, sem_kp, chunk, n_groups):
    slot = chunk % PAGE_RING
    b = chunk // n_groups
    g = chunk % n_groups
    copies = []
    for i in range(PAGES_PER_STEP):
        pg = pt_ref[b, g * PAGES_PER_STEP + i]
        copies.append(pltpu.make_async_copy(ckv_hbm.at[pg], ck_ring.at[slot, i], sem_ck.at[slot]))
        copies.append(pltpu.make_async_copy(kpt_hbm.at[pg], kp_ring.at[slot, i], sem_kp.at[slot]))
    return copies


def _decode_kernel(pt_ref, ql_ref, qp_ref, cn_ref, kn_ref, ckv_hbm, kpt_hbm, o_ref,
                   ck_ring, kp_ring, sem_ck, sem_kp, ck_all, s_all):
    g = pl.program_id(1)
    n_groups = pl.num_programs(1)
    n_chunks = pl.num_programs(0) * n_groups
    chunk = pl.program_id(0) * n_groups + g
    page = ck_ring.shape[2]
    ring_args = (pt_ref, ckv_hbm, kpt_hbm, ck_ring, kp_ring, sem_ck, sem_kp)

    @pl.when(chunk == 0)
    def _():
        for c in range(PAGE_RING - 1):
            for cp in _page_copies(*ring_args, jnp.int32(c), n_groups):
                cp.start()

    @pl.when(chunk + (PAGE_RING - 1) < n_chunks)
    def _():
        for cp in _page_copies(*ring_args, chunk + (PAGE_RING - 1), n_groups):
            cp.start()

    for cp in _page_copies(*ring_args, chunk, n_groups):
        cp.wait()

    slot = chunk % PAGE_RING
    q = ql_ref[0]
    qp = qp_ref[0]
    for i in range(PAGES_PER_STEP):
        ck_page = ck_ring[slot, i].astype(BF16)
        ck_all[g, i * page:(i + 1) * page, :] = ck_page
        s_all[g, :, i * page:(i + 1) * page] = (
            lax.dot_general(q, ck_page, _NT, preferred_element_type=F32)
            + jnp.dot(qp, kp_ring[slot, i].astype(BF16), preferred_element_type=F32))

    @pl.when(g == n_groups - 1)
    def _():
        n_g = s_all.shape[0]
        c_new = cn_ref[0].astype(BF16).astype(F32)
        k_new = kn_ref[0].astype(BF16).astype(F32)
        s_self = (jnp.sum(q.astype(F32) * c_new, axis=-1, keepdims=True)
                  + jnp.sum(qp.astype(F32) * k_new, axis=-1, keepdims=True))
        m_fin = s_self
        for j in range(n_g):
            m_fin = jnp.maximum(m_fin, jnp.max(s_all[j], axis=-1, keepdims=True))
        p_self = jnp.exp2(s_self - m_fin)
        l_fin = p_self
        acc = p_self.astype(BF16).astype(F32) * c_new
        for j in range(n_g):
            p = jnp.exp2(s_all[j] - m_fin)
            l_fin = l_fin + jnp.sum(p, axis=-1, keepdims=True)
            acc = acc + jnp.dot(p.astype(BF16), ck_all[j], preferred_element_type=F32)
        o_ref[0] = (acc / l_fin).astype(o_ref.dtype)


def decode_attention(page_table, q_lat, q_pe, c_new, kpe_new, cache_ckv, cache_kpe_t):
    n_b, n_h, r = q_lat.shape
    p_dim = q_pe.shape[2]
    page = cache_ckv.shape[1]
    n_pages = page_table.shape[1]
    g_pages = PAGES_PER_STEP
    n_groups = n_pages // g_pages
    assert n_groups * g_pages == n_pages and n_b * n_groups >= PAGE_RING - 1

    in_specs = [pl.BlockSpec((1, n_h, r), lambda b, g, pt: (b, 0, 0)),
                pl.BlockSpec((1, n_h, p_dim), lambda b, g, pt: (b, 0, 0)),
                pl.BlockSpec((1, 1, r), lambda b, g, pt: (b, 0, 0)),
                pl.BlockSpec((1, 1, p_dim), lambda b, g, pt: (b, 0, 0)),
                pl.BlockSpec(memory_space=pl.ANY),
                pl.BlockSpec(memory_space=pl.ANY)]
    grid_spec = pltpu.PrefetchScalarGridSpec(
        num_scalar_prefetch=1, grid=(n_b, n_groups), in_specs=in_specs,
        out_specs=pl.BlockSpec((1, n_h, r), lambda b, g, pt: (b, 0, 0)),
        scratch_shapes=[pltpu.VMEM((PAGE_RING, g_pages, page, r), F32),
                        pltpu.VMEM((PAGE_RING, g_pages, p_dim, page), F32),
                        pltpu.SemaphoreType.DMA((PAGE_RING,)),
                        pltpu.SemaphoreType.DMA((PAGE_RING,)),
                        pltpu.VMEM((n_groups, g_pages * page, r), BF16),
                        pltpu.VMEM((n_groups, n_h, g_pages * page), F32)])
    return pl.pallas_call(
        _decode_kernel, grid_spec=grid_spec,
        out_shape=jax.ShapeDtypeStruct((n_b, n_h, r), BF16),
        compiler_params=_params(2), name="decode_attention")(
            page_table, q_lat, q_pe, c_new, kpe_new, cache_ckv, cache_kpe_t)


def _comb_identity(accs, ex):
    return (accs[0],)


def _comb_glu(accs, ex):
    return ((accs[0] + ex[0]) * _sigmoid(accs[1] + ex[1]),)


def _comb_short_conv_in(accs, ex):
    return (accs[0], accs[1] * accs[2])


def _comb_rms(accs, ex):
    y = _rms(accs[0], ex[0])
    return (y, y)


def _tile_lanes(tab, width):
    reps = width // tab.shape[1]
    return tab if reps == 1 else jnp.concatenate([tab] * reps, axis=-1)


def _swap_lane_halves(a):
    groups = [pltpu.roll(a[:, c * LANES:(c + 1) * LANES], LANES // 2, 1)
              for c in range(a.shape[1] // LANES)]
    return groups[0] if len(groups) == 1 else jnp.concatenate(groups, axis=-1)


def _comb_rope(scale, accs, ex):
    a = accs[0]
    cos_t = _tile_lanes(ex[0], a.shape[1])
    sin_t = _tile_lanes(ex[1], a.shape[1])
    return ((a * cos_t + _swap_lane_halves(a) * sin_t) * scale,)


def _comb_rope_f32_bf16(accs, ex):
    y = accs[0] * ex[0] + _swap_lane_halves(accs[0]) * ex[1]
    return (y, y)


def _comb_scale(scale, accs, ex):
    return (accs[0] * scale,)


def _rot_cols(w):
    half = w.shape[-1] // 2
    return jnp.concatenate([-w[..., half:], w[..., :half]], axis=-1)


def _last_rows(u, n_batch, seq, n_keep):
    return jnp.stack([u[(b + 1) * seq - n_keep:(b + 1) * seq] for b in range(n_batch)])


def _pad_lanes(w):
    pad = LANES - w.shape[-1]
    return jnp.concatenate([w, jnp.zeros(w.shape[:-1] + (pad,), w.dtype)], axis=-1)


def kernel(x_prompt, x_sample, state_conv_l0, cache_ckv_l1, cache_kpe_l1, state_sconv_l2, state_conv_l3, page_table, norm_g, ca_w_in, ca_b_in, ca_dw, ca_dw_b, ca_ln_g, ca_ln_b, ca_w_out, ca_b_out, mla_w_dq, mla_q_norm, mla_w_uq, mla_w_dkv, mla_kv_norm, mla_w_uk, mla_w_uv, mla_w_o, sc_w_in, sc_w_conv, sc_w_out, ffn_w_gu, ffn_w_down):
    b_p, s_p, d = x_prompt.shape
    b_s = x_sample.shape[0]
    m_p = b_p * s_p
    depth = norm_g.shape[0]
    past_len = page_table.shape[1] * cache_ckv_l1.shape[1]
    conv_state_in = (state_conv_l0, state_conv_l3)
    tm = ROW_TILE
    sample_block = m_p // b_s

    gains = norm_g.reshape(depth * 4, 1, d)
    ca_b_in3 = ca_b_in[:, None, :]
    ca_b_out3 = ca_b_out[:, None, :]
    ca_dw_b3 = ca_dw_b[:, None, :]
    ca_ln_g3 = ca_ln_g[:, None, :]
    ca_ln_b3 = ca_ln_b[:, None, :]

    x, h = stack_and_norm(x_prompt.reshape(m_p, d), x_sample.reshape(b_s, d), gains, 0, tm=tm)

    half = QK_ROPE_DIM // 2
    inv = ROPE_THETA ** (-jnp.arange(half, dtype=F32) / half)
    pos = jnp.concatenate([jnp.arange(s_p), jnp.full((1,), past_len, jnp.int32)])
    ang = pos.astype(F32)[:, None] * inv[None, :]

    def rope_table(t):
        t = _pad_lanes(jnp.concatenate([t, t], axis=-1))
        return jnp.concatenate([jnp.tile(t[:s_p], (b_p, 1)), jnp.tile(t[s_p:], (b_s, 1))], axis=0)

    cos_tab = rope_table(jnp.cos(ang))
    sin_tab = rope_table(jnp.sin(ang))

    outs_conv = []
    for i in range(depth):
        kind = i % N_MIXERS
        g_post, g_ffn = 4 * i + 1, 4 * i + 2
        if kind == 0:
            j = i // N_MIXERS
            nb = d // 512
            (u,) = multi_mm("conf_in", h, [(ca_w_in, j, 0), (ca_w_in, j, nb)],
                            [(ca_b_in3, "col", j, 0), (ca_b_in3, "col", j, nb)],
                            _comb_glu, [F32], n_cols=d, tm=tm, tn=512)
            v = conf_conv_prompt(u, ca_dw, ca_dw_b3, ca_ln_g3, ca_ln_b3, j, batch=b_p, seq=s_p, ts=256)
            state_t = jnp.transpose(conv_state_in[j], (1, 0, 2))
            v, new_state_t = conf_conv_sample(v, state_t, u, ca_dw, ca_dw_b3, ca_ln_g3, ca_ln_b3, j,
                                              first_row=m_p, tb=16)
            x, h = out_proj("conf_out", v, ca_w_out, j, ca_b_out3, x, gains, g_post, g_ffn,
                            tm=OUT_PROJ_TILE)
            conv_p = _last_rows(u, b_p, s_p, ca_dw.shape[1] - 1)
            conv_s = jnp.transpose(new_state_t, (1, 0, 2))
            outs_conv.append((conv_p, conv_s))
        elif kind == 1:
            r = KV_LORA_RANK
            qr = mla_w_dq.shape[1]
            w_down = jnp.concatenate([mla_w_dq, mla_w_dkv[:, :r]], axis=1)[None]
            g_down = jnp.concatenate([mla_q_norm, mla_kv_norm])[None, None]
            lat_f32, lat_bf = multi_mm("mla_down", h, [(w_down, 0, 0)], [(g_down, "col", 0, 0)],
                                       _comb_rms, [F32, BF16], n_cols=qr + r, tm=tm, tn=512)
            w_kpe = mla_w_dkv[:, r:]
            w_kpe2 = jnp.concatenate([w_kpe, _rot_cols(w_kpe)], axis=-1)[None]
            kpe, kpe_bf = multi_mm("mla_kpe", h, [(w_kpe2, 0, 0)],
                                   [(cos_tab, "row", None, None), (sin_tab, "row", None, None)],
                                   _comb_rope_f32_bf16, [F32, BF16], n_cols=LANES, tm=tm, tn=LANES)
            w_uq3 = mla_w_uq.reshape(qr, N_HEADS, QK_NOPE_DIM + QK_ROPE_DIM)
            w_qn = w_uq3[:, :, :QK_NOPE_DIM].reshape(1, qr, N_HEADS * QK_NOPE_DIM)
            w_qp = w_uq3[:, :, QK_NOPE_DIM:]
            w_qp2 = jnp.concatenate([w_qp, _rot_cols(w_qp)], axis=-1).reshape(1, qr, N_HEADS * LANES)
            (q_nope,) = multi_mm("mla_qn", lat_bf, [(w_qn, 0, 0)], [],
                                 functools.partial(_comb_scale, Q_SCALE), [BF16],
                                 n_cols=N_HEADS * QK_NOPE_DIM, tm=tm, tn=1024, x_cols=(0, qr))
            (q_pe,) = multi_mm("mla_qp", lat_bf, [(w_qp2, 0, 0)],
                               [(cos_tab, "row", None, None), (sin_tab, "row", None, None)],
                               functools.partial(_comb_rope, Q_SCALE), [BF16],
                               n_cols=N_HEADS * LANES, tm=tm, tn=1024, x_cols=(0, qr))
            w_uk2 = mla_w_uk.reshape(r, N_HEADS * QK_NOPE_DIM)
            w_uv2 = mla_w_uv.reshape(r, N_HEADS * V_HEAD_DIM)
            w_up = jnp.concatenate([w_uk2, w_uv2], axis=1)[None]
            (kv_up,) = multi_mm("mla_kv_up", lat_bf, [(w_up, 0, 0)], [], _comb_identity, [BF16],
                                n_cols=w_up.shape[2], tm=tm, tn=1024, x_cols=(qr // r, r))
            att = flash_prompt(q_nope, q_pe, kv_up, kpe_bf, batch=b_p, seq=s_p)
            q_lat = absorb_q(q_nope, w_uk2, row_block=sample_block, n_rows=b_s).transpose(1, 0, 2)
            q_pe_s = q_pe[m_p:].reshape(b_s, N_HEADS, LANES)[:, :, :QK_ROPE_DIM]
            ckv_s = lat_f32[m_p:, qr:].reshape(b_s, 1, r)
            kpe_s = kpe[m_p:, :QK_ROPE_DIM].reshape(b_s, 1, QK_ROPE_DIM)
            cache_kpe_t = jnp.swapaxes(cache_kpe_l1, 1, 2)
            o_lat = decode_attention(page_table, q_lat, q_pe_s, ckv_s, kpe_s, cache_ckv_l1, cache_kpe_t)
            att = expand_o(att, o_lat.transpose(1, 0, 2), w_uv2, row_block=sample_block)
            x, h = out_proj("mla_out", att, mla_w_o[None], 0, None, x, gains, g_post, g_ffn,
                            tm=OUT_PROJ_TILE)
            ckv_p = lat_f32[:m_p, qr:].reshape(b_p, s_p, r)
            kpe_p = kpe[:m_p, :QK_ROPE_DIM].reshape(b_p, s_p, QK_ROPE_DIM)
        else:
            nb = d // 512
            sc_w_in3 = sc_w_in[None]
            gate, u = multi_mm("sc_in", h, [(sc_w_in3, 0, 0), (sc_w_in3, 0, nb), (sc_w_in3, 0, 2 * nb)], [],
                               _comb_short_conv_in, [F32, F32], n_cols=d, tm=tm, tn=512)
            v = short_conv_prompt(u, gate, sc_w_conv, batch=b_p, seq=s_p, ts=256)
            v = short_conv_sample(v, state_sconv_l2, u, gate, sc_w_conv, first_row=m_p, tb=32)
            x, h = out_proj("sc_out", v, sc_w_out[None], 0, None, x, gains, g_post, g_ffn,
                            tm=OUT_PROJ_TILE)
            n_keep = sc_w_conv.shape[0] - 1
            u_s = u[m_p:]
            sc_p = _last_rows(u, b_p, s_p, n_keep)
            sc_s = jnp.concatenate([state_sconv_l2[:, 1:], u_s[:, None, :]], axis=1)

        g_next = 4 * (i + 1) if i + 1 < depth else None
        x, h = ffn(h, x, ffn_w_gu, ffn_w_down, i, gains, 4 * i + 3, g_next,
                   tm=tm, tf=FFN_CHUNK, n_tail=b_s)

    y_prompt = x.reshape(b_p, s_p, d)
    y_sample = h.reshape(b_s, 1, d)
    return (y_prompt, y_sample, outs_conv[0][0], outs_conv[0][1], ckv_p, kpe_p,
            ckv_s, kpe_s, sc_p, sc_s, outs_conv[1][0], outs_conv[1][1])
```
